```python
import jax, jax.numpy as jnp
from jax import lax
import numpy as np

D_MODEL = 2048
BATCH = 4
SEQ = 2048
DEPTH = 4
DEC_BATCH = 128
DEC_SEQ = 4
PAST_LEN = 16384
PAGE_SIZE = 128

D_MIX = D_MODEL
W_A = D_MIX // 2
W_B = D_MIX - W_A
HGRN_EXPAND = 128
H_A = W_A // HGRN_EXPAND
DK = HGRN_EXPAND
DV = W_A // H_A
CONV_W = 3
N_META = 16
CHUNK = 64
EPS = 1e-6
F_FLOOR = 1e-30
SPLITS = (W_A, W_A, W_A, W_A, W_B, W_B, W_B, W_B)
D_IN = sum(SPLITS)

kernel_name = "hymba_hgrn2_shortconv_decode_step"


def rmsnorm(x, w):
    xf = x.astype(jnp.float32)
    y = xf * lax.rsqrt(jnp.mean(xf * xf, axis=-1, keepdims=True) + EPS)
    return (y * w.astype(jnp.float32)).astype(x.dtype)


def hgrn_chunk(S, inp):
    q, k, g, v = inp
    L = q.shape[1]
    G = jnp.cumsum(g, axis=1)
    causal = jnp.tril(jnp.ones((L, L), dtype=bool))[None, :, :, None, None]
    diff = G[:, :, None] - G[:, None, :]
    decay = jnp.where(causal, jnp.exp(jnp.minimum(diff, 0.0)), 0.0)
    A = jnp.sum(q[:, :, None] * k[:, None] * decay, axis=-1)
    o = (jnp.einsum('btsh,bshv->bthv', A, v)
         + jnp.einsum('bthk,bhkv->bthv', q * jnp.exp(G), S))
    G_last = G[:, -1]
    S_new = (jnp.exp(G_last)[..., None] * S
             + jnp.einsum('bshk,bshv->bhkv', k * jnp.exp(G_last[:, None] - G), v))
    return S_new, o


def hgrn_prompt(q, k, g, v):
    Bn = q.shape[0]
    S0 = jnp.zeros((Bn, H_A, DK, DV), jnp.float32)
    S, o_meta = hgrn_chunk(S0, (q[:, :N_META], k[:, :N_META], g[:, :N_META], v[:, :N_META]))

    def to_chunks(a):
        r = a[:, N_META:]
        n = r.shape[1] // CHUNK
        return r.reshape(Bn, n, CHUNK, *r.shape[2:]).swapaxes(0, 1)

    S, o_rest = lax.scan(hgrn_chunk, S, (to_chunks(q), to_chunks(k), to_chunks(g), to_chunks(v)))
    o_rest = o_rest.swapaxes(0, 1).reshape(Bn, -1, H_A, DV)
    return jnp.concatenate([o_meta, o_rest], axis=1), S


def short_conv(u, buf, w):
    T = u.shape[1]
    pad = jnp.concatenate([buf.astype(u.dtype), u], axis=1)
    y = sum(w[j] * pad[:, j:j + T] for j in range(CONV_W))
    return y, pad[:, -(CONV_W - 1):]


def mixer_layer(h, l, S0, buf, prompt, w_in, conv_w, lb_param, hgrn_norm_w, w_out,
                pre_norm_w, post_norm_w):
    Bn, T, _ = h.shape
    hn = rmsnorm(h, pre_norm_w[l])
    proj = jnp.einsum('btd,de->bte', hn, w_in[l]).astype(jnp.float32)
    q, fz, i, za, bg, cg, xt, zb = jnp.split(proj, list(np.cumsum(SPLITS)[:-1]), axis=-1)

    lb_w = jax.nn.softmax(lb_param.astype(jnp.float32), axis=0)
    lb_all = jnp.cumsum(lb_w, axis=0) - lb_w[0]
    lb = lb_all[l].reshape(H_A, DK)
    fz = fz.reshape(Bn, T, H_A, DK)
    f = lb + (1.0 - lb) * jax.nn.sigmoid(fz)
    log_f = jnp.log(jnp.maximum(f, F_FLOOR))
    k = (1.0 - lb) * jax.nn.sigmoid(-fz)
    q = q.reshape(Bn, T, H_A, DK)
    v = i.reshape(Bn, T, H_A, DV)
    if prompt:
        o, S = hgrn_prompt(q, k, log_f, v)
    else:
        S, o = hgrn_chunk(S0.astype(jnp.float32), (q, k, log_f, v))
    o = rmsnorm(o, hgrn_norm_w[l]).reshape(Bn, T, W_A) * jax.nn.silu(za)

    if prompt:
        buf = jnp.zeros((Bn, CONV_W - 1, W_B), jnp.float32)
    c, new_buf = short_conv(cg * xt, buf, conv_w[l].astype(jnp.float32))
    yb = bg * c * jax.nn.silu(zb)

    mix = jnp.einsum('bte,ed->btd', jnp.concatenate([o, yb], axis=-1).astype(h.dtype), w_out[l])
    h = h + rmsnorm(mix, post_norm_w[l])
    return h, S, new_buf


def setup_inputs(seed: int = 0) -> dict:
    key = jax.random.key(seed)
    ks = jax.random.split(key, 12)
    f32 = jnp.float32
    return {
        "x_prompt": jax.random.normal(ks[0], (BATCH, SEQ, D_MODEL), f32),
        "x_sample": jax.random.normal(ks[1], (DEC_BATCH, DEC_SEQ, D_MODEL), f32),
        "state_hgrn": 0.5 * jax.random.normal(ks[2], (DEPTH, DEC_BATCH, H_A, DK, DV), f32),
        "state_conv": jax.random.normal(ks[3], (DEPTH, DEC_BATCH, CONV_W - 1, W_B), f32),
        "meta_tokens": jax.random.normal(ks[4], (N_META, D_MODEL), f32),
        "w_in": jax.random.normal(ks[5], (DEPTH, D_MODEL, D_IN), f32) * D_MODEL ** -0.5,
        "conv_w": jax.random.normal(ks[6], (DEPTH, CONV_W, W_B), f32) * CONV_W ** -0.5,
        "lb_param": 0.5 * jax.random.normal(ks[7], (DEPTH, H_A * DK), f32),
        "hgrn_norm_w": 1.0 + 0.02 * jax.random.normal(ks[8], (DEPTH, DV), f32),
        "w_out": jax.random.normal(ks[9], (DEPTH, D_MIX, D_MODEL), f32) * D_MIX ** -0.5,
        "pre_norm_w": 1.0 + 0.02 * jax.random.normal(ks[10], (DEPTH, D_MODEL), f32),
        "post_norm_w": 1.0 + 0.02 * jax.random.normal(ks[11], (DEPTH, D_MODEL), f32),
    }


def reference(x_prompt, x_sample, state_hgrn, state_conv, meta_tokens, w_in, conv_w, lb_param,
              hgrn_norm_w, w_out, pre_norm_w, post_norm_w):
    Bp = x_prompt.shape[0]
    meta = jnp.broadcast_to(meta_tokens[None].astype(x_prompt.dtype), (Bp, N_META, D_MODEL))
    hp = jnp.concatenate([meta, x_prompt], axis=1)
    hs = x_sample
    Sp_list, bp_list, Ss_list, bs_list = [], [], [], []
    for l in range(DEPTH):
        hp, Sp, bp = mixer_layer(hp, l, None, None, True, w_in, conv_w, lb_param, hgrn_norm_w,
                                 w_out, pre_norm_w, post_norm_w)
        hs, Ss, bs = mixer_layer(hs, l, state_hgrn[l], state_conv[l], False, w_in, conv_w,
                                 lb_param, hgrn_norm_w, w_out, pre_norm_w, post_norm_w)
        Sp_list.append(Sp.astype(x_prompt.dtype))
        bp_list.append(bp.astype(x_prompt.dtype))
        Ss_list.append(Ss.astype(state_hgrn.dtype))
        bs_list.append(bs.astype(state_conv.dtype))
    y_prompt = hp[:, N_META:]
    y_sample = hs
    new_hgrn_prompt = jnp.stack(Sp_list)
    new_conv_prompt = jnp.stack(bp_list)
    new_hgrn_sample = jnp.stack(Ss_list)
    new_conv_sample = jnp.stack(bs_list)
    return (y_prompt, y_sample, new_hgrn_prompt, new_conv_prompt, new_hgrn_sample, new_conv_sample)
```

```python
import functools

import jax
import jax.numpy as jnp
from jax import lax
from jax.experimental import pallas as pl
from jax.experimental.pallas import tpu as pltpu

LANES = 128
SUBLANES = 8
N_GROUPS = 8
CONV_W = 3
EPS = 1e-6
F_FLOOR = 1e-30
PROMPT_TILE = 512
PROMPT_CHUNK = 128
SAMPLE_TILE = 256
VMEM_LIMIT = 56 * 1024 * 1024

_F32 = jnp.float32
_BF16 = jnp.bfloat16
_NT = (((1,), (1,)), ((), ()))


def _sigmoid(x):
    return 1.0 / (1.0 + jnp.exp(-x))


def _silu(x):
    return x * _sigmoid(x)


def _rms_scale(x):
    return lax.rsqrt(jnp.mean(x * x, axis=-1, keepdims=True) + EPS)


def _lower_bound(lbp, layer):
    e = jnp.exp(lbp - jnp.max(lbp, axis=0, keepdims=True))
    w = e / jnp.sum(e, axis=0, keepdims=True)
    r = lax.broadcasted_iota(jnp.int32, w.shape, 0)
    return jnp.sum(jnp.where((r >= 1) & (r <= layer), w, 0.0), axis=0, keepdims=True)


def _forget_gates(fz, lb):
    one_m = 1.0 - lb
    f = lb + one_m * _sigmoid(fz)
    return jnp.log(jnp.maximum(f, F_FLOOR)), one_m * _sigmoid(-fz)


def _shift_rows(x, s):
    return pltpu.roll(x, s % x.shape[0], 0)


def _cumsum_rows(x, row):
    s = 1
    while s < x.shape[0]:
        x = x + jnp.where(row >= s, _shift_rows(x, s), 0.0)
        s *= 2
    return x


def _anchor_rows(g_ref, G, n, row):
    C = G.shape[0]
    half = n // 2
    if n == 2:
        return jnp.where((row & 1) == 1, _shift_rows(G, 1), G)
    if n == 4:
        pos = row & 3
        return jnp.where(pos == 0, _shift_rows(G, -1),
                         jnp.where(pos == 1, G,
                                   jnp.where(pos == 2, _shift_rows(G, 1), _shift_rows(G, 2))))
    parts = []
    for a in range(C // n):
        r = g_ref[pl.ds(a * n + half - 1, 1), :]
        parts.append(jnp.broadcast_to(r, (n, LANES)))
    return parts[0] if len(parts) == 1 else jnp.concatenate(parts, axis=0)


def _intra_chunk_scores(q, kk, G, g_ref, row, col):
    C = q.shape[0]
    A = jnp.where(row == col, jnp.sum(q * kk, axis=-1, keepdims=True), 0.0)
    n = 2
    while n <= C:
        half = n // 2
        E = jnp.exp(-jnp.abs(G - _anchor_rows(g_ref, G, n, row)))
        P = lax.dot_general((q * E).astype(_BF16), (kk * E).astype(_BF16), _NT,
                            preferred_element_type=_F32)
        shift = n.bit_length() - 1
        mask = ((row >> shift) == (col >> shift)) & ((row & (n - 1)) >= half) & ((col & (n - 1)) < half)
        A = jnp.where(mask, P, A)
        n *= 2
    return A


def _conv_gate(bg, u, um1, um2, zb, cw):
    c = cw[0:1] * um2 + cw[1:2] * um1 + cw[2:3] * u
    return bg * c * _silu(zb)


def _pre_norm(h, pre_ref, hn_ref, acc_ref):
    hn_ref[...] = (h * _rms_scale(h) * pre_ref[...]).astype(_BF16)
    acc_ref[...] = jnp.zeros_like(acc_ref)


def _post_norm(h, acc_ref, post_ref):
    mix = acc_ref[...]
    return h + mix * _rms_scale(mix) * post_ref[...]


def _group(ref, g, rows=None):
    cols = slice(g * LANES, (g + 1) * LANES)
    return ref[:, cols] if rows is None else ref[rows, cols]


def _prompt_kernel(x_ref, win_ref, wout_ref, lbp_ref, nw_ref, cw_ref, pre_ref, post_ref,
                   sinit_ref, cinit_ref,
                   out_ref, sout_ref, cout_ref,
                   hn_ref, acc_ref, proj_ref, oy_ref, st_ref, cb_ref, g_ref, *carry,
                   layer0, chunk, tiles_per_seq):
    l, i, j = pl.program_id(0), pl.program_id(1), pl.program_id(2)
    n_heads = pl.num_programs(2)
    TM = x_ref.shape[0]
    C = chunk
    h_ref = carry[0] if carry else None

    def tile_input():
        return x_ref[...] if h_ref is None else h_ref[i]

    if h_ref is not None:
        @pl.when((j == 0) & (l == 0))
        def _():
            h_ref[i] = x_ref[...]

    @pl.when(j == 0)
    def _():
        _pre_norm(tile_input(), pre_ref, hn_ref, acc_ref)

    @pl.when(i % tiles_per_seq == 0)
    def _():
        st_ref[j] = sinit_ref[...].T
        cb_ref[j] = cinit_ref[...]

    proj_ref[...] = jnp.dot(hn_ref[...], win_ref[...], preferred_element_type=_F32)

    lb = _lower_bound(lbp_ref[...], layer0 + l)
    nw = nw_ref[...]
    row = lax.broadcasted_iota(jnp.int32, (C, 1), 0)
    col = lax.broadcasted_iota(jnp.int32, (1, C), 1)

    for c in range(TM // C):
        rows = slice(c * C, (c + 1) * C)
        q = _group(proj_ref, 0, rows)
        v = _group(proj_ref, 2, rows)
        g, kk = _forget_gates(_group(proj_ref, 1, rows), lb)
        G = _cumsum_rows(g, row)
        g_ref[...] = G
        A = _intra_chunk_scores(q, kk, G, g_ref, row, col)
        St = st_ref[j]
        v16 = v.astype(_BF16)
        o = jnp.dot(A.astype(_BF16), v16, preferred_element_type=_F32)
        o = o + lax.dot_general((q * jnp.exp(G)).astype(_BF16), St.astype(_BF16), _NT,
                                preferred_element_type=_F32)
        GL = G[C - 1:C]
        khat = (kk * jnp.exp(GL - G)).astype(_BF16)
        st_ref[j] = St * jnp.exp(GL) + pl.dot(v16, khat, trans_a=True)
        o = o * _rms_scale(o) * nw * _silu(_group(proj_ref, 3, rows))
        oy_ref[rows, 0:LANES] = o

    u = _group(proj_ref, 5) * _group(proj_ref, 6)
    prev = cb_ref[j]
    trow = lax.broadcasted_iota(jnp.int32, (TM, 1), 0)
    um1 = jnp.where(trow >= 1, _shift_rows(u, 1), prev[7:8])
    um2 = jnp.where(trow >= 2, _shift_rows(u, 2), jnp.where(trow == 0, prev[6:7], prev[7:8]))
    oy_ref[:, LANES:2 * LANES] = _conv_gate(_group(proj_ref, 4), u, um1, um2,
                                            _group(proj_ref, 7), cw_ref[...])
    tail = u[TM - SUBLANES:TM]
    cb_ref[j] = tail
    cout_ref[...] = tail
    sout_ref[...] = st_ref[j].T

    acc_ref[...] += jnp.dot(oy_ref[...].astype(_BF16), wout_ref[...], preferred_element_type=_F32)

    @pl.when(j == n_heads - 1)
    def _():
        h_new = _post_norm(tile_input(), acc_ref, post_ref)
        out_ref[...] = h_new
        if h_ref is not None:
            h_ref[i] = h_new


def _prompt_stream(x, w_in, w_out, lb_param, hgrn_norm_w, conv_w, pre_w, post_w, s_init, c_init,
                   *, layer0, n_layers, tile, chunk, tiles_per_seq):
    R, D = x.shape
    H = w_out.shape[1] // (2 * LANES)
    n_tiles = R // tile
    depth = lb_param.shape[0]
    carry = n_layers > 1
    grid = (n_layers, n_tiles, H)
    kern = functools.partial(_prompt_kernel, layer0=layer0, chunk=chunk, tiles_per_seq=tiles_per_seq)
    in_specs = [
        pl.BlockSpec((tile, D), lambda l, i, j: (i, 0)),
        pl.BlockSpec((None, D, N_GROUPS * LANES), lambda l, i, j: (layer0 + l, 0, j)),
        pl.BlockSpec((None, 2 * LANES, D), lambda l, i, j: (layer0 + l, j, 0)),
        pl.BlockSpec((depth, LANES), lambda l, i, j: (0, j)),
        pl.BlockSpec((None, 1, LANES), lambda l, i, j: (layer0 + l, 0, 0)),
        pl.BlockSpec((None, CONV_W, LANES), lambda l, i, j: (layer0 + l, 0, j)),
        pl.BlockSpec((None, 1, D), lambda l, i, j: (layer0 + l, 0, 0)),
        pl.BlockSpec((None, 1, D), lambda l, i, j: (layer0 + l, 0, 0)),
        pl.BlockSpec((None, None, LANES, LANES), lambda l, i, j: (l, j, 0, 0)),
        pl.BlockSpec((None, SUBLANES, LANES), lambda l, i, j: (l, 0, j)),
    ]
    out_specs = [
        pl.BlockSpec((None, tile, D), lambda l, i, j: (l, i, 0)),
        pl.BlockSpec((None, None, None, LANES, LANES), lambda l, i, j: (l, i, j, 0, 0)),
        pl.BlockSpec((None, None, SUBLANES, LANES), lambda l, i, j: (l, i, 0, j)),
    ]
    out_shape = [
        jax.ShapeDtypeStruct((n_layers, R, D), _F32),
        jax.ShapeDtypeStruct((n_layers, n_tiles, H, LANES, LANES), _F32),
        jax.ShapeDtypeStruct((n_layers, n_tiles, SUBLANES, H * LANES), _F32),
    ]
    scratch = [
        pltpu.VMEM((tile, D), _BF16),
        pltpu.VMEM((tile, D), _F32),
        pltpu.VMEM((tile, N_GROUPS * LANES), _F32),
        pltpu.VMEM((tile, 2 * LANES), _F32),
        pltpu.VMEM((H, LANES, LANES), _F32),
        pltpu.VMEM((H, SUBLANES, LANES), _F32),
        pltpu.VMEM((chunk, LANES), _F32),
    ]
    if carry:
        scratch.append(pltpu.VMEM((n_tiles, tile, D), _F32))
    return pl.pallas_call(
        kern, grid=grid, in_specs=in_specs, out_specs=out_specs, out_shape=out_shape,
        scratch_shapes=scratch,
        compiler_params=pltpu.CompilerParams(
            dimension_semantics=("arbitrary", "arbitrary", "arbitrary"),
            vmem_limit_bytes=VMEM_LIMIT),
        name="prompt_stream" if not carry else "meta_stream",
    )(x, w_in, w_out, lb_param, hgrn_norm_w, conv_w, pre_w, post_w, s_init, c_init)


def _sample_kernel(x_ref, win_ref, wout_ref, lbp_ref, nw_ref, cw_ref, pre_ref, post_ref,
                   sin_ref, cpad_ref,
                   out_ref, sout_ref, uout_ref,
                   hn_ref, acc_ref, proj_ref, oy_ref, qt_ref, kh_ref, dec_ref, oi_ref, h_ref,
                   *, seq_len):
    l, i, j = pl.program_id(0), pl.program_id(1), pl.program_id(2)
    n_heads = pl.num_programs(2)
    TM = x_ref.shape[0]
    T = seq_len
    per_vreg = SUBLANES // T

    def tile_input():
        return h_ref[i]

    @pl.when((j == 0) & (l == 0))
    def _():
        h_ref[i] = x_ref[...]

    @pl.when(j == 0)
    def _():
        _pre_norm(tile_input(), pre_ref, hn_ref, acc_ref)

    proj_ref[...] = jnp.dot(hn_ref[...], win_ref[...], preferred_element_type=_F32)

    lb = _lower_bound(lbp_ref[...], l)
    row = lax.broadcasted_iota(jnp.int32, (TM, 1), 0)
    pos = row % T
    q = _group(proj_ref, 0)
    v = _group(proj_ref, 2)
    g, kk = _forget_gates(_group(proj_ref, 1), lb)

    G = g
    for d in range(1, T):
        G = G + jnp.where(pos >= d, _shift_rows(g, d), 0.0)
    o = jnp.sum(q * kk, axis=-1, keepdims=True) * v
    for d in range(1, T):
        decay = jnp.exp(jnp.minimum(G - _shift_rows(G, d), 0.0))
        a = jnp.sum(jnp.where(pos >= d, q * _shift_rows(kk, d) * decay, 0.0), axis=-1, keepdims=True)
        o = o + a * _shift_rows(v, d)
    GL = G
    for d in range(1, T):
        GL = jnp.where(pos == T - 1 - d, _shift_rows(G, -d), GL)
    qt_ref[...] = q * jnp.exp(G)
    kh_ref[...] = kk * jnp.exp(GL - G)
    dec_ref[...] = jnp.exp(GL)

    row8 = lax.broadcasted_iota(jnp.int32, (SUBLANES, 1), 0)

    def group_body(p, carry):
        r = pl.multiple_of(p * SUBLANES, SUBLANES)
        q8 = qt_ref[pl.ds(r, SUBLANES), :].astype(_BF16)
        k8 = kh_ref[pl.ds(r, SUBLANES), :]
        v8 = proj_ref[pl.ds(r, SUBLANES), 2 * LANES:3 * LANES].astype(_BF16)
        d8 = dec_ref[pl.ds(r, SUBLANES), :]
        o8 = jnp.zeros((SUBLANES, LANES), _F32)
        for s in range(per_vreg):
            b = p * per_vreg + s
            mine = (row8 >= s * T) & (row8 < (s + 1) * T)
            S0 = sin_ref[b]
            o8 = jnp.where(mine, jnp.dot(q8, S0.astype(_BF16), preferred_element_type=_F32), o8)
            upd = pl.dot(jnp.where(mine, k8, 0.0).astype(_BF16), v8, trans_a=True)
            decay = jnp.broadcast_to(d8[s * T:s * T + 1], (LANES, LANES)).T
            sout_ref[b] = decay * S0 + upd
        oi_ref[pl.ds(r, SUBLANES), :] = o8
        return carry

    lax.fori_loop(0, TM // SUBLANES, group_body, 0)

    o = o + oi_ref[...]
    oy_ref[:, 0:LANES] = o * _rms_scale(o) * nw_ref[...] * _silu(_group(proj_ref, 3))

    u = _group(proj_ref, 5) * _group(proj_ref, 6)
    e = cpad_ref[...]
    um1 = jnp.where(pos >= 1, _shift_rows(u, 1), _shift_rows(e, -1))
    um2 = jnp.where(pos >= 2, _shift_rows(u, 2), e)
    oy_ref[:, LANES:2 * LANES] = _conv_gate(_group(proj_ref, 4), u, um1, um2,
                                            _group(proj_ref, 7), cw_ref[...])
    uout_ref[...] = u

    acc_ref[...] += jnp.dot(oy_ref[...].astype(_BF16), wout_ref[...], preferred_element_type=_F32)

    @pl.when(j == n_heads - 1)
    def _():
        h_new = _post_norm(tile_input(), acc_ref, post_ref)
        out_ref[...] = h_new
        h_ref[i] = h_new


def _sample_stream(x, w_in, w_out, lb_param, hgrn_norm_w, conv_w, pre_w, post_w, state, cpad,
                   *, seq_len, tile):
    R, D = x.shape
    depth, n_seq, H = state.shape[:3]
    n_tiles = R // tile
    seq_per_tile = tile // seq_len
    grid = (depth, n_tiles, H)
    kern = functools.partial(_sample_kernel, seq_len=seq_len)
    in_specs = [
        pl.BlockSpec((tile, D), lambda l, i, j: (i, 0)),
        pl.BlockSpec((None, D, N_GROUPS * LANES), lambda l, i, j: (l, 0, j)),
        pl.BlockSpec((None, 2 * LANES, D), lambda l, i, j: (l, j, 0)),
        pl.BlockSpec((depth, LANES), lambda l, i, j: (0, j)),
        pl.BlockSpec((None, 1, LANES), lambda l, i, j: (l, 0, 0)),
        pl.BlockSpec((None, CONV_W, LANES), lambda l, i, j: (l, 0, j)),
        pl.BlockSpec((None, 1, D), lambda l, i, j: (l, 0, 0)),
        pl.BlockSpec((None, 1, D), lambda l, i, j: (l, 0, 0)),
        pl.BlockSpec((None, seq_per_tile, None, LANES, LANES), lambda l, i, j: (l, i, j, 0, 0)),
        pl.BlockSpec((None, tile, LANES), lambda l, i, j: (l, i, j)),
    ]
    out_specs = [
        pl.BlockSpec((None, tile, D), lambda l, i, j: (l, i, 0)),
        pl.BlockSpec((None, seq_per_tile, None, LANES, LANES), lambda l, i, j: (l, i, j, 0, 0)),
        pl.BlockSpec((None, tile, LANES), lambda l, i, j: (l, i, j)),
    ]
    out_shape = [
        jax.ShapeDtypeStruct((depth, R, D), _F32),
        jax.ShapeDtypeStruct(state.shape, _F32),
        jax.ShapeDtypeStruct((depth, R, H * LANES), _F32),
    ]
    scratch = [
        pltpu.VMEM((tile, D), _BF16),
        pltpu.VMEM((tile, D), _F32),
        pltpu.VMEM((tile, N_GROUPS * LANES), _F32),
        pltpu.VMEM((tile, 2 * LANES), _F32),
        pltpu.VMEM((tile, LANES), _F32),
        pltpu.VMEM((tile, LANES), _F32),
        pltpu.VMEM((tile, LANES), _F32),
        pltpu.VMEM((tile, LANES), _F32),
        pltpu.VMEM((n_tiles, tile, D), _F32),
    ]
    return pl.pallas_call(
        kern, grid=grid, in_specs=in_specs, out_specs=out_specs, out_shape=out_shape,
        scratch_shapes=scratch,
        compiler_params=pltpu.CompilerParams(
            dimension_semantics=("arbitrary", "arbitrary", "arbitrary"),
            vmem_limit_bytes=VMEM_LIMIT),
        name="sample_stream",
    )(x, w_in, w_out, lb_param, hgrn_norm_w, conv_w, pre_w, post_w, state, cpad)


def kernel(x_prompt, x_sample, state_hgrn, state_conv, meta_tokens, w_in, conv_w, lb_param,
           hgrn_norm_w, w_out, pre_norm_w, post_norm_w):
    B, T, D = x_prompt.shape
    DB, DT, _ = x_sample.shape
    depth = w_in.shape[0]
    n_meta = meta_tokens.shape[0]
    W = D // 2
    H = W // LANES
    assert W % LANES == 0 and SUBLANES % DT == 0 and n_meta % SUBLANES == 0
    assert w_in.shape == (depth, D, N_GROUPS * W) and state_conv.shape[2] == CONV_W - 1

    w_in_h = (w_in.reshape(depth, D, N_GROUPS, H, LANES).transpose(0, 1, 3, 2, 4)
              .reshape(depth, D, H * N_GROUPS * LANES).astype(_BF16))
    w_out_h = (w_out.reshape(depth, 2, H, LANES, D).transpose(0, 2, 1, 3, 4)
               .reshape(depth, 2 * W, D).astype(_BF16))
    nw = hgrn_norm_w.reshape(depth, 1, LANES)
    pre = pre_norm_w.reshape(depth, 1, D)
    post = post_norm_w.reshape(depth, 1, D)
    params = (w_in_h, w_out_h, lb_param, nw, conv_w, pre, post)

    zeros_s = jnp.zeros((depth, H, LANES, LANES), _F32)
    zeros_c = jnp.zeros((depth, SUBLANES, W), _F32)
    _, s_meta, c_meta = _prompt_stream(
        meta_tokens, *params, zeros_s, zeros_c, layer0=0, n_layers=depth,
        tile=n_meta, chunk=n_meta, tiles_per_seq=1)

    sample_tile = min(SAMPLE_TILE, DB * DT)
    cpad = jnp.concatenate(
        [state_conv, jnp.zeros((depth, DB, DT - (CONV_W - 1), W), _F32)], axis=2
    ).reshape(depth, DB * DT, W)
    hs, s_sample, u_sample = _sample_stream(
        x_sample.reshape(DB * DT, D), *params, state_hgrn, cpad, seq_len=DT, tile=sample_tile)
    y_sample = hs[depth - 1].reshape(DB, DT, D)
    c_sample = u_sample.reshape(depth, DB, DT, W)[:, :, DT - (CONV_W - 1):]

    tile = min(PROMPT_TILE, T)
    chunk = min(PROMPT_CHUNK, tile)
    hp = x_prompt.reshape(B * T, D)
    s_list, c_list = [], []
    for layer in range(depth):
        hp3, s_p, c_p = _prompt_stream(
            hp, *params, s_meta[layer], c_meta[layer], layer0=layer, n_layers=1,
            tile=tile, chunk=chunk, tiles_per_seq=T // tile)
        hp = hp3[0]
        last = slice(T // tile - 1, None, T // tile)
        s_list.append(s_p[0, last])
        c_list.append(c_p[0, last, SUBLANES - (CONV_W - 1):])
    y_prompt = hp.reshape(B, T, D)
    return (y_prompt, y_sample, jnp.stack(s_list), jnp.stack(c_list), s_sample, c_sample)
```

```python
import functools

import jax
import jax.numpy as jnp
from jax import lax
from jax.experimental import pallas as pl
from jax.experimental.pallas import tpu as pltpu

LANES = 128
SUBLANES = 8
N_GROUPS = 8
CONV_W = 3
EPS = 1e-6
F_FLOOR = 1e-30
PROMPT_TILE = 512
PROMPT_CHUNK = 128
SAMPLE_TILE = 128
SAMPLE_UNROLL = 4
VMEM_LIMIT = 56 * 1024 * 1024

_F32 = jnp.float32
_BF16 = jnp.bfloat16
_NT = (((1,), (1,)), ((), ()))
_ARBITRARY3 = ("arbitrary", "arbitrary", "arbitrary")


def _sigmoid(x):
    return 1.0 / (1.0 + jnp.exp(-x))


def _silu(x):
    return x * _sigmoid(x)


def _rms_scale(x):
    return lax.rsqrt(jnp.mean(x * x, axis=-1, keepdims=True) + EPS)


def _lower_bound(lbp, layer):
    e = jnp.exp(lbp - jnp.max(lbp, axis=0, keepdims=True))
    w = e / jnp.sum(e, axis=0, keepdims=True)
    r = lax.broadcasted_iota(jnp.int32, w.shape, 0)
    return jnp.sum(jnp.where((r >= 1) & (r <= layer), w, 0.0), axis=0, keepdims=True)


def _forget_gates(fz, lb):
    one_m = 1.0 - lb
    f = lb + one_m * _sigmoid(fz)
    return jnp.log(jnp.maximum(f, F_FLOOR)), one_m * _sigmoid(-fz)


def _shift_rows(x, s):
    return pltpu.roll(x, s % x.shape[0], 0)


def _cumsum_rows(x, row):
    s = 1
    while s < x.shape[0]:
        x = x + jnp.where(row >= s, _shift_rows(x, s), 0.0)
        s *= 2
    return x


def _anchor_rows(g_ref, G, n, row):
    C = G.shape[0]
    half = n // 2
    if n == 2:
        return jnp.where((row & 1) == 1, _shift_rows(G, 1), G)
    if n == 4:
        pos = row & 3
        return jnp.where(pos == 0, _shift_rows(G, -1),
                         jnp.where(pos == 1, G,
                                   jnp.where(pos == 2, _shift_rows(G, 1), _shift_rows(G, 2))))
    parts = []
    for a in range(C // n):
        r = g_ref[pl.ds(a * n + half - 1, 1), :]
        parts.append(jnp.broadcast_to(r, (n, LANES)))
    return parts[0] if len(parts) == 1 else jnp.concatenate(parts, axis=0)


def _intra_chunk_scores(q, kk, G, g_ref, row, col):
    C = q.shape[0]
    A = jnp.where(row == col, jnp.sum(q * kk, axis=-1, keepdims=True), 0.0)
    n = 2
    while n <= C:
        half = n // 2
        E = jnp.exp(-jnp.abs(G - _anchor_rows(g_ref, G, n, row)))
        P = lax.dot_general((q * E).astype(_BF16), (kk * E).astype(_BF16), _NT,
                            preferred_element_type=_F32)
        shift = n.bit_length() - 1
        mask = ((row >> shift) == (col >> shift)) & ((row & (n - 1)) >= half) & ((col & (n - 1)) < half)
        A = jnp.where(mask, P, A)
        n *= 2
    return A


def _group(ref, g, rows=slice(None)):
    return ref[rows, g * LANES:(g + 1) * LANES]


def _conv_gate(proj_ref, u, um1, um2, cw):
    c = cw[0:1] * um2 + cw[1:2] * um1 + cw[2:3] * u
    return _group(proj_ref, 4) * c * _silu(_group(proj_ref, 7))


def _cast_column_groups(slab_refs, w_scr):
    for g, slab in enumerate(slab_refs):
        w_scr[:, g * LANES:(g + 1) * LANES] = slab[...].astype(_BF16)


def _chunked_head_mixer(proj_ref, lb, nw, cw, st_ref, cb_ref, g_ref, chunk, o_dst, o_cols, yb_dst, yb_cols):
    TM = proj_ref.shape[0]
    C = chunk
    row = lax.broadcasted_iota(jnp.int32, (C, 1), 0)
    col = lax.broadcasted_iota(jnp.int32, (1, C), 1)
    for c in range(TM // C):
        rows = slice(c * C, (c + 1) * C)
        q = _group(proj_ref, 0, rows)
        v16 = _group(proj_ref, 2, rows).astype(_BF16)
        g, kk = _forget_gates(_group(proj_ref, 1, rows), lb)
        G = _cumsum_rows(g, row)
        g_ref[...] = G
        A = _intra_chunk_scores(q, kk, G, g_ref, row, col)
        St = st_ref[...]
        o = jnp.dot(A.astype(_BF16), v16, preferred_element_type=_F32)
        o = o + lax.dot_general((q * jnp.exp(G)).astype(_BF16), St.astype(_BF16), _NT,
                                preferred_element_type=_F32)
        GL = G[C - 1:C]
        khat = (kk * jnp.exp(GL - G)).astype(_BF16)
        st_ref[...] = St * jnp.exp(GL) + pl.dot(v16, khat, trans_a=True)
        o = o * _rms_scale(o) * nw * _silu(_group(proj_ref, 3, rows))
        o_dst[rows, o_cols] = o.astype(o_dst.dtype)

    u = _group(proj_ref, 5) * _group(proj_ref, 6)
    prev = cb_ref[...]
    trow = lax.broadcasted_iota(jnp.int32, (TM, 1), 0)
    um1 = jnp.where(trow >= 1, _shift_rows(u, 1), prev[7:8])
    um2 = jnp.where(trow >= 2, _shift_rows(u, 2), jnp.where(trow == 0, prev[6:7], prev[7:8]))
    yb_dst[:, yb_cols] = _conv_gate(proj_ref, u, um1, um2, cw).astype(yb_dst.dtype)
    tail = u[TM - SUBLANES:TM]
    cb_ref[...] = tail
    return tail


def _in_slab_specs(D, H, layer_of, head_of):
    return [pl.BlockSpec((None, D, LANES),
                         lambda *ids, g=g: (layer_of(*ids), 0, g * H + head_of(*ids)))
            for g in range(N_GROUPS)]


def _norm_kernel(x_ref, w_ref, o_ref):
    x = x_ref[...]
    o_ref[...] = (x * _rms_scale(x) * w_ref[...]).astype(_BF16)


def _prompt_norm(x, pre_w, layer, tile):
    R, D = x.shape
    return pl.pallas_call(
        _norm_kernel, grid=(R // tile,),
        in_specs=[pl.BlockSpec((tile, D), lambda i: (i, 0)),
                  pl.BlockSpec((None, 1, D), lambda i: (layer, 0, 0))],
        out_specs=pl.BlockSpec((tile, D), lambda i: (i, 0)),
        out_shape=jax.ShapeDtypeStruct((R, D), _BF16),
        compiler_params=pltpu.CompilerParams(dimension_semantics=("arbitrary",),
                                             vmem_limit_bytes=VMEM_LIMIT),
        name="prompt_norm",
    )(x, pre_w)


def _prompt_mixer_kernel(hn_ref, *refs, layer, chunk, tiles_per_seq):
    slabs = refs[:N_GROUPS]
    (lbp_ref, nw_ref, cw_ref, sinit_ref, cinit_ref,
     o_ref, yb_ref, sout_ref, cout_ref,
     w_scr, proj_ref, st_ref, cb_ref, g_ref) = refs[N_GROUPS:]
    i = pl.program_id(1)

    @pl.when(i == 0)
    def _():
        _cast_column_groups(slabs, w_scr)

    @pl.when(i % tiles_per_seq == 0)
    def _():
        st_ref[...] = sinit_ref[...].T
        cb_ref[...] = cinit_ref[...]

    proj_ref[...] = jnp.dot(hn_ref[...], w_scr[...], preferred_element_type=_F32)
    lb = _lower_bound(lbp_ref[...], layer)
    full = slice(None)
    tail = _chunked_head_mixer(proj_ref, lb, nw_ref[...], cw_ref[...], st_ref, cb_ref, g_ref, chunk,
                               o_ref, full, yb_ref, full)
    cout_ref[...] = tail
    sout_ref[...] = st_ref[...].T


def _prompt_mixer(hn, w_in, lb_param, nw, conv_w, s_init, c_init, *, layer, tile, chunk, tiles_per_seq):
    R, D = hn.shape
    W = w_in.shape[2] // N_GROUPS
    H = W // LANES
    depth = lb_param.shape[0]
    n_tiles = R // tile
    kern = functools.partial(_prompt_mixer_kernel, layer=layer, chunk=chunk, tiles_per_seq=tiles_per_seq)
    in_specs = (
        [pl.BlockSpec((tile, D), lambda j, i: (i, 0))]
        + _in_slab_specs(D, H, lambda j, i: layer, lambda j, i: j)
        + [pl.BlockSpec((depth, LANES), lambda j, i: (0, j)),
           pl.BlockSpec((None, 1, LANES), lambda j, i: (layer, 0, 0)),
           pl.BlockSpec((None, CONV_W, LANES), lambda j, i: (layer, 0, j)),
           pl.BlockSpec((None, LANES, LANES), lambda j, i: (j, 0, 0)),
           pl.BlockSpec((SUBLANES, LANES), lambda j, i: (0, j))])
    out_specs = [
        pl.BlockSpec((tile, LANES), lambda j, i: (i, j)),
        pl.BlockSpec((tile, LANES), lambda j, i: (i, j)),
        pl.BlockSpec((None, None, LANES, LANES), lambda j, i: (i, j, 0, 0)),
        pl.BlockSpec((None, SUBLANES, LANES), lambda j, i: (i, 0, j)),
    ]
    out_shape = [
        jax.ShapeDtypeStruct((R, W), _BF16),
        jax.ShapeDtypeStruct((R, W), _BF16),
        jax.ShapeDtypeStruct((n_tiles, H, LANES, LANES), _F32),
        jax.ShapeDtypeStruct((n_tiles, SUBLANES, W), _F32),
    ]
    scratch = [
        pltpu.VMEM((D, N_GROUPS * LANES), _BF16),
        pltpu.VMEM((tile, N_GROUPS * LANES), _F32),
        pltpu.VMEM((LANES, LANES), _F32),
        pltpu.VMEM((SUBLANES, LANES), _F32),
        pltpu.VMEM((chunk, LANES), _F32),
    ]
    return pl.pallas_call(
        kern, grid=(H, n_tiles), in_specs=in_specs, out_specs=out_specs, out_shape=out_shape,
        scratch_shapes=scratch,
        compiler_params=pltpu.CompilerParams(dimension_semantics=("arbitrary", "arbitrary"),
                                             vmem_limit_bytes=VMEM_LIMIT),
        name="prompt_mixer",
    )(hn, *([w_in] * N_GROUPS), lb_param, nw, conv_w, s_init, c_init)


def _prompt_out_kernel(h_ref, o_ref, yb_ref, wout_ref, post_ref, *refs, emit_next):
    if emit_next:
        pre_ref, hout_ref, hn_ref, wo_scr = refs
    else:
        hout_ref, wo_scr = refs

    @pl.when(pl.program_id(0) == 0)
    def _():
        wo_scr[...] = wout_ref[...].astype(_BF16)

    W = o_ref.shape[1]
    mix = (jnp.dot(o_ref[...], wo_scr[0:W], preferred_element_type=_F32)
           + jnp.dot(yb_ref[...], wo_scr[W:2 * W], preferred_element_type=_F32))
    h = h_ref[...] + mix * _rms_scale(mix) * post_ref[...]
    hout_ref[...] = h
    if emit_next:
        hn_ref[...] = (h * _rms_scale(h) * pre_ref[...]).astype(_BF16)


def _prompt_out(h, o, yb, w_out, post_w, pre_w, *, layer, tile, emit_next):
    R, D = h.shape
    W = o.shape[1]
    kern = functools.partial(_prompt_out_kernel, emit_next=emit_next)
    in_specs = [
        pl.BlockSpec((tile, D), lambda i: (i, 0)),
        pl.BlockSpec((tile, W), lambda i: (i, 0)),
        pl.BlockSpec((tile, W), lambda i: (i, 0)),
        pl.BlockSpec((None, 2 * W, D), lambda i: (layer, 0, 0), pipeline_mode=pl.Buffered(1)),
        pl.BlockSpec((None, 1, D), lambda i: (layer, 0, 0)),
    ]
    out_specs = [pl.BlockSpec((tile, D), lambda i: (i, 0))]
    out_shape = [jax.ShapeDtypeStruct((R, D), _F32)]
    args = [h, o, yb, w_out, post_w]
    if emit_next:
        in_specs.append(pl.BlockSpec((None, 1, D), lambda i: (layer + 1, 0, 0)))
        out_specs.append(pl.BlockSpec((tile, D), lambda i: (i, 0)))
        out_shape.append(jax.ShapeDtypeStruct((R, D), _BF16))
        args.append(pre_w)
    return pl.pallas_call(
        kern, grid=(R // tile,), in_specs=in_specs, out_specs=out_specs, out_shape=out_shape,
        scratch_shapes=[pltpu.VMEM((2 * W, D), _BF16)],
        compiler_params=pltpu.CompilerParams(dimension_semantics=("arbitrary",),
                                             vmem_limit_bytes=VMEM_LIMIT),
        name="prompt_out",
    )(*args)


def _small_stream_prologue(x_ref, slabs, wo_a_ref, wo_b_ref, pre_ref, h_ref, hn_ref, acc_ref,
                           w_scr, wo_scr, proj_ref):
    l, j, i = pl.program_id(0), pl.program_id(1), pl.program_id(2)

    @pl.when(i == 0)
    def _():
        _cast_column_groups(slabs, w_scr)
        wo_scr[0:LANES] = wo_a_ref[...].astype(_BF16)
        wo_scr[LANES:2 * LANES] = wo_b_ref[...].astype(_BF16)

    @pl.when((j == 0) & (l == 0))
    def _():
        h_ref[i] = x_ref[...]

    @pl.when(j == 0)
    def _():
        h = h_ref[i]
        hn_ref[i] = (h * _rms_scale(h) * pre_ref[...]).astype(_BF16)
        acc_ref[i] = jnp.zeros(acc_ref.shape[1:], _F32)

    proj_ref[...] = jnp.dot(hn_ref[i], w_scr[...], preferred_element_type=_F32)


def _is_final_step(l, j):
    return (l == pl.num_programs(0) - 1) & (j == pl.num_programs(1) - 1)


def _small_stream_epilogue(oy_ref, wo_scr, acc_ref, h_ref, post_ref, out_ref):
    l, j, i = pl.program_id(0), pl.program_id(1), pl.program_id(2)
    acc_ref[i] += jnp.dot(oy_ref[...], wo_scr[...], preferred_element_type=_F32)

    @pl.when(j == pl.num_programs(1) - 1)
    def _():
        mix = acc_ref[i]
        h_ref[i] = h_ref[i] + mix * _rms_scale(mix) * post_ref[...]

    @pl.when(_is_final_step(l, j))
    def _():
        out_ref[...] = h_ref[i]

    @pl.when(jnp.logical_not(_is_final_step(l, j)))
    def _():
        out_ref[...] = jnp.zeros_like(out_ref)


def _meta_kernel(x_ref, *refs, chunk):
    slabs = refs[:N_GROUPS]
    (wo_a_ref, wo_b_ref, lbp_ref, nw_ref, cw_ref, pre_ref, post_ref,
     out_ref, sout_ref, cout_ref,
     hn_ref, acc_ref, h_ref, w_scr, wo_scr, proj_ref, oy_ref, st_ref, cb_ref, g_ref) = refs[N_GROUPS:]
    _small_stream_prologue(x_ref, slabs, wo_a_ref, wo_b_ref, pre_ref, h_ref, hn_ref, acc_ref,
                           w_scr, wo_scr, proj_ref)
    st_ref[...] = jnp.zeros_like(st_ref)
    cb_ref[...] = jnp.zeros_like(cb_ref)
    lb = _lower_bound(lbp_ref[...], pl.program_id(0))
    tail = _chunked_head_mixer(proj_ref, lb, nw_ref[...], cw_ref[...], st_ref, cb_ref, g_ref, chunk,
                               oy_ref, slice(0, LANES), oy_ref, slice(LANES, 2 * LANES))
    cout_ref[...] = tail
    sout_ref[...] = st_ref[...].T
    _small_stream_epilogue(oy_ref, wo_scr, acc_ref, h_ref, post_ref, out_ref)


def _sample_kernel(x_ref, *refs, seq_len):
    slabs = refs[:N_GROUPS]
    (wo_a_ref, wo_b_ref, lbp_ref, nw_ref, cw_ref, pre_ref, post_ref, sin_ref, cpad_ref,
     out_ref, sout_ref, uout_ref,
     hn_ref, acc_ref, h_ref, w_scr, wo_scr, proj_ref, oy_ref,
     qt_ref, kh_ref, dec_ref, oi_ref) = refs[N_GROUPS:]
    _small_stream_prologue(x_ref, slabs, wo_a_ref, wo_b_ref, pre_ref, h_ref, hn_ref, acc_ref,
                           w_scr, wo_scr, proj_ref)
    TM = x_ref.shape[0]
    T = seq_len
    per_group = SUBLANES // T

    lb = _lower_bound(lbp_ref[...], pl.program_id(0))
    row = lax.broadcasted_iota(jnp.int32, (TM, 1), 0)
    pos = row % T
    q = _group(proj_ref, 0)
    v = _group(proj_ref, 2)
    g, kk = _forget_gates(_group(proj_ref, 1), lb)

    G = g
    for d in range(1, T):
        G = G + jnp.where(pos >= d, _shift_rows(g, d), 0.0)
    o = jnp.sum(q * kk, axis=-1, keepdims=True) * v
    for d in range(1, T):
        decay = jnp.exp(jnp.minimum(G - _shift_rows(G, d), 0.0))
        a = jnp.sum(jnp.where(pos >= d, q * _shift_rows(kk, d) * decay, 0.0), axis=-1, keepdims=True)
        o = o + a * _shift_rows(v, d)
    GL = G
    for d in range(1, T):
        GL = jnp.where(pos == T - 1 - d, _shift_rows(G, -d), GL)
    qt_ref[...] = q * jnp.exp(G)
    kh_ref[...] = kk * jnp.exp(GL - G)
    dec_ref[...] = jnp.exp(GL)

    row8 = lax.broadcasted_iota(jnp.int32, (SUBLANES, 1), 0)

    def group_body(p, carry):
        r = pl.multiple_of(p * SUBLANES, SUBLANES)
        q8 = qt_ref[pl.ds(r, SUBLANES), :].astype(_BF16)
        k8 = kh_ref[pl.ds(r, SUBLANES), :]
        v8 = proj_ref[pl.ds(r, SUBLANES), 2 * LANES:3 * LANES].astype(_BF16)
        d8 = dec_ref[pl.ds(r, SUBLANES), :]
        o8 = jnp.zeros((SUBLANES, LANES), _F32)
        for s in range(per_group):
            b = p * per_group + s
            mine = (row8 >= s * T) & (row8 < (s + 1) * T)
            S0 = sin_ref[b]
            o8 = jnp.where(mine, jnp.dot(q8, S0.astype(_BF16), preferred_element_type=_F32), o8)
            upd = pl.dot(jnp.where(mine, k8, 0.0).astype(_BF16), v8, trans_a=True)
            decay = jnp.broadcast_to(d8[s * T:s * T + 1], (LANES, LANES)).T
            sout_ref[b] = decay * S0 + upd
        oi_ref[pl.ds(r, SUBLANES), :] = o8
        return carry

    lax.fori_loop(0, TM // SUBLANES, group_body, 0, unroll=SAMPLE_UNROLL)

    o = o + oi_ref[...]
    oy_ref[:, 0:LANES] = (o * _rms_scale(o) * nw_ref[...] * _silu(_group(proj_ref, 3))).astype(_BF16)

    u = _group(proj_ref, 5) * _group(proj_ref, 6)
    e = cpad_ref[...]
    um1 = jnp.where(pos >= 1, _shift_rows(u, 1), _shift_rows(e, -1))
    um2 = jnp.where(pos >= 2, _shift_rows(u, 2), e)
    oy_ref[:, LANES:2 * LANES] = _conv_gate(proj_ref, u, um1, um2, cw_ref[...]).astype(_BF16)
    uout_ref[...] = u
    _small_stream_epilogue(oy_ref, wo_scr, acc_ref, h_ref, post_ref, out_ref)


def _small_stream(kern, name, x, w_in, w_out, lb_param, nw, conv_w, pre_w, post_w, extra_in,
                  extra_in_specs, extra_out_shapes, extra_out_specs, extra_scratch, *, tile):
    R, D = x.shape
    depth = w_in.shape[0]
    W = w_in.shape[2] // N_GROUPS
    H = W // LANES
    n_tiles = R // tile
    in_specs = (
        [pl.BlockSpec((tile, D), lambda l, j, i: (i, 0))]
        + _in_slab_specs(D, H, lambda l, j, i: l, lambda l, j, i: j)
        + [pl.BlockSpec((None, LANES, D), lambda l, j, i: (l, j, 0)),
           pl.BlockSpec((None, LANES, D), lambda l, j, i: (l, H + j, 0)),
           pl.BlockSpec((depth, LANES), lambda l, j, i: (0, j)),
           pl.BlockSpec((None, 1, LANES), lambda l, j, i: (l, 0, 0)),
           pl.BlockSpec((None, CONV_W, LANES), lambda l, j, i: (l, 0, j)),
           pl.BlockSpec((None, 1, D), lambda l, j, i: (l, 0, 0)),
           pl.BlockSpec((None, 1, D), lambda l, j, i: (l, 0, 0))]
        + extra_in_specs)
    def out_block(l, j, i):
        final = (l == depth - 1) & (j == H - 1)
        return (jnp.where(final, i, n_tiles), 0)

    out_specs = [pl.BlockSpec((tile, D), out_block)] + extra_out_specs
    out_shape = [jax.ShapeDtypeStruct((R + tile, D), _F32)] + extra_out_shapes
    scratch = [
        pltpu.VMEM((n_tiles, tile, D), _BF16),
        pltpu.VMEM((n_tiles, tile, D), _F32),
        pltpu.VMEM((n_tiles, tile, D), _F32),
        pltpu.VMEM((D, N_GROUPS * LANES), _BF16),
        pltpu.VMEM((2 * LANES, D), _BF16),
        pltpu.VMEM((tile, N_GROUPS * LANES), _F32),
        pltpu.VMEM((tile, 2 * LANES), _BF16),
    ] + extra_scratch
    return pl.pallas_call(
        kern, grid=(depth, H, n_tiles), in_specs=in_specs, out_specs=out_specs, out_shape=out_shape,
        scratch_shapes=scratch,
        compiler_params=pltpu.CompilerParams(dimension_semantics=_ARBITRARY3,
                                             vmem_limit_bytes=VMEM_LIMIT),
        name=name,
    )(x, *([w_in] * N_GROUPS), w_out, w_out, lb_param, nw, conv_w, pre_w, post_w, *extra_in)


def kernel(x_prompt, x_sample, state_hgrn, state_conv, meta_tokens, w_in, conv_w, lb_param,
           hgrn_norm_w, w_out, pre_norm_w, post_norm_w):
    B, T, D = x_prompt.shape
    DB, DT, _ = x_sample.shape
    depth = w_in.shape[0]
    n_meta = meta_tokens.shape[0]
    W = D // 2
    H = W // LANES
    assert W % LANES == 0 and SUBLANES % DT == 0 and n_meta % (2 * SUBLANES) == 0
    assert w_in.shape == (depth, D, N_GROUPS * W) and state_conv.shape[2] == CONV_W - 1

    nw = hgrn_norm_w.reshape(depth, 1, LANES)
    pre = pre_norm_w.reshape(depth, 1, D)
    post = post_norm_w.reshape(depth, 1, D)
    params = (w_in, w_out, lb_param, nw, conv_w, pre, post)

    state_tile_spec = pl.BlockSpec((None, None, None, LANES, LANES), lambda l, j, i: (l, i, j, 0, 0))
    _, s_meta, c_meta = _small_stream(
        functools.partial(_meta_kernel, chunk=n_meta), "meta_stream", meta_tokens, *params,
        extra_in=[], extra_in_specs=[],
        extra_out_shapes=[jax.ShapeDtypeStruct((depth, 1, H, LANES, LANES), _F32),
                          jax.ShapeDtypeStruct((depth, 1, SUBLANES, W), _F32)],
        extra_out_specs=[state_tile_spec,
                         pl.BlockSpec((None, None, SUBLANES, LANES), lambda l, j, i: (l, i, 0, j))],
        extra_scratch=[pltpu.VMEM((LANES, LANES), _F32), pltpu.VMEM((SUBLANES, LANES), _F32),
                       pltpu.VMEM((n_meta, LANES), _F32)],
        tile=n_meta)

    tile_s = min(SAMPLE_TILE, DB * DT)
    seq_per_tile = tile_s // DT
    cpad = jnp.concatenate(
        [state_conv, jnp.zeros((depth, DB, DT - (CONV_W - 1), W), _F32)], axis=2
    ).reshape(depth, DB * DT, W)
    seq_state_spec = pl.BlockSpec((None, seq_per_tile, None, LANES, LANES), lambda l, j, i: (l, i, j, 0, 0))
    row_cols_spec = pl.BlockSpec((None, tile_s, LANES), lambda l, j, i: (l, i, j))
    hs, s_sample, u_sample = _small_stream(
        functools.partial(_sample_kernel, seq_len=DT), "sample_stream", x_sample.reshape(DB * DT, D), *params,
        extra_in=[state_hgrn, cpad], extra_in_specs=[seq_state_spec, row_cols_spec],
        extra_out_shapes=[jax.ShapeDtypeStruct(state_hgrn.shape, _F32),
                          jax.ShapeDtypeStruct((depth, DB * DT, W), _F32)],
        extra_out_specs=[seq_state_spec, row_cols_spec],
        extra_scratch=[pltpu.VMEM((tile_s, LANES), _F32)] * 4,
        tile=tile_s)
    y_sample = hs[:DB * DT].reshape(DB, DT, D)
    c_sample = u_sample.reshape(depth, DB, DT, W)[:, :, DT - (CONV_W - 1):]

    tile = min(PROMPT_TILE, T)
    chunk = min(PROMPT_CHUNK, tile)
    tiles_per_seq = T // tile
    last = slice(tiles_per_seq - 1, None, tiles_per_seq)
    hp = x_prompt.reshape(B * T, D)
    hn = _prompt_norm(hp, pre, 0, tile)
    s_list, c_list = [], []
    for layer in range(depth):
        o, yb, s_p, c_p = _prompt_mixer(hn, w_in, lb_param, nw, conv_w, s_meta[layer, 0], c_meta[layer, 0],
                                        layer=layer, tile=tile, chunk=chunk, tiles_per_seq=tiles_per_seq)
        emit_next = layer + 1 < depth
        outs = _prompt_out(hp, o, yb, w_out, post, pre, layer=layer, tile=tile, emit_next=emit_next)
        hp = outs[0]
        hn = outs[1] if emit_next else None
        s_list.append(s_p[last])
        c_list.append(c_p[last, SUBLANES - (CONV_W - 1):])
    y_prompt = hp.reshape(B, T, D)
    return (y_prompt, y_sample, jnp.stack(s_list), jnp.stack(c_list), s_sample, c_sample)
```

```python
import functools

import jax
import jax.numpy as jnp
from jax import lax
from jax.experimental import pallas as pl
from jax.experimental.pallas import tpu as pltpu

LANES = 128
SUBLANES = 8
N_GROUPS = 8
CONV_W = 3
EPS = 1e-6
F_FLOOR = 1e-30
ROW_TILE = 512
OUT_TILE = 256
PROMPT_CHUNK = 128
SAMPLE_PIECE = 128
SAMPLE_UNROLL = 4
VMEM_LIMIT = 56 * 1024 * 1024

_F32 = jnp.float32
_BF16 = jnp.bfloat16
_NT = (((1,), (1,)), ((), ()))


def _sigmoid(x):
    return 1.0 / (1.0 + jnp.exp(-x))


def _silu(x):
    return x * _sigmoid(x)


def _rms_scale(x):
    return lax.rsqrt(jnp.mean(x * x, axis=-1, keepdims=True) + EPS)


def _lower_bound(lbp, layer):
    e = jnp.exp(lbp - jnp.max(lbp, axis=0, keepdims=True))
    w = e / jnp.sum(e, axis=0, keepdims=True)
    r = lax.broadcasted_iota(jnp.int32, w.shape, 0)
    return jnp.sum(jnp.where((r >= 1) & (r <= layer), w, 0.0), axis=0, keepdims=True)


def _forget_gates(fz, lb):
    one_m = 1.0 - lb
    f = lb + one_m * _sigmoid(fz)
    return jnp.log(jnp.maximum(f, F_FLOOR)), one_m * _sigmoid(-fz)


def _shift_rows(x, s):
    return pltpu.roll(x, s % x.shape[0], 0)


def _cumsum_rows(x, row):
    s = 1
    while s < x.shape[0]:
        x = x + jnp.where(row >= s, _shift_rows(x, s), 0.0)
        s *= 2
    return x


def _anchor_rows(g_ref, G, n, row):
    C = G.shape[0]
    half = n // 2
    if n == 2:
        return jnp.where((row & 1) == 1, _shift_rows(G, 1), G)
    if n == 4:
        pos = row & 3
        return jnp.where(pos == 0, _shift_rows(G, -1),
                         jnp.where(pos == 1, G,
                                   jnp.where(pos == 2, _shift_rows(G, 1), _shift_rows(G, 2))))
    parts = []
    for a in range(C // n):
        r = g_ref[pl.ds(a * n + half - 1, 1), :]
        parts.append(jnp.broadcast_to(r, (n, LANES)))
    return parts[0] if len(parts) == 1 else jnp.concatenate(parts, axis=0)


def _intra_chunk_scores(q, kk, G, g_ref, row, col):
    C = q.shape[0]
    A = jnp.where(row == col, jnp.sum(q * kk, axis=-1, keepdims=True), 0.0)
    n = 2
    while n <= C:
        half = n // 2
        E = jnp.exp(-jnp.abs(G - _anchor_rows(g_ref, G, n, row)))
        P = lax.dot_general((q * E).astype(_BF16), (kk * E).astype(_BF16), _NT,
                            preferred_element_type=_F32)
        shift = n.bit_length() - 1
        mask = ((row >> shift) == (col >> shift)) & ((row & (n - 1)) >= half) & ((col & (n - 1)) < half)
        A = jnp.where(mask, P, A)
        n *= 2
    return A


def _group(ref, g, rows=slice(None)):
    return ref[rows, g * LANES:(g + 1) * LANES]


def _conv_gate(bg, zb, u, um1, um2, cw):
    c = cw[0:1] * um2 + cw[1:2] * um1 + cw[2:3] * u
    return bg * c * _silu(zb)


def _cast_column_groups(slab_refs, w_scr):
    for g, slab in enumerate(slab_refs):
        w_scr[:, g * LANES:(g + 1) * LANES] = slab[...].astype(_BF16)


def _chunked_head_mixer(proj_ref, lb, nw, cw, st_ref, cb_ref, g_ref, chunk, o_ref, yb_ref, before_chunk=None):
    TM = proj_ref.shape[0]
    C = chunk
    row = lax.broadcasted_iota(jnp.int32, (C, 1), 0)
    col = lax.broadcasted_iota(jnp.int32, (1, C), 1)
    for c in range(TM // C):
        if before_chunk is not None:
            before_chunk(c, TM // C)
        rows = slice(c * C, (c + 1) * C)
        q = _group(proj_ref, 0, rows)
        v16 = _group(proj_ref, 2, rows).astype(_BF16)
        g, kk = _forget_gates(_group(proj_ref, 1, rows), lb)
        G = _cumsum_rows(g, row)
        g_ref[0:C] = G
        A = _intra_chunk_scores(q, kk, G, g_ref, row, col)
        St = st_ref[...]
        o = jnp.dot(A.astype(_BF16), v16, preferred_element_type=_F32)
        o = o + lax.dot_general((q * jnp.exp(G)).astype(_BF16), St.astype(_BF16), _NT,
                                preferred_element_type=_F32)
        GL = G[C - 1:C]
        khat = (kk * jnp.exp(GL - G)).astype(_BF16)
        st_ref[...] = St * jnp.exp(GL) + pl.dot(v16, khat, trans_a=True)
        o = o * _rms_scale(o) * nw * _silu(_group(proj_ref, 3, rows))
        o_ref[rows, :] = o.astype(o_ref.dtype)

    u = _group(proj_ref, 5) * _group(proj_ref, 6)
    prev = cb_ref[...]
    trow = lax.broadcasted_iota(jnp.int32, (TM, 1), 0)
    um1 = jnp.where(trow >= 1, _shift_rows(u, 1), prev[7:8])
    um2 = jnp.where(trow >= 2, _shift_rows(u, 2), jnp.where(trow == 0, prev[6:7], prev[7:8]))
    yb_ref[...] = _conv_gate(_group(proj_ref, 4), _group(proj_ref, 7), u, um1, um2, cw).astype(yb_ref.dtype)
    cb_ref[...] = u[TM - SUBLANES:TM]


def _sample_head_mixer(proj_ref, rows, lb, nw, cw, sin_ref, sout_ref, cpad_ref, o_ref, yb_ref, u_ref,
                       qt_ref, kh_ref, dec_ref, oi_ref, seq_len):
    T = seq_len
    TM = qt_ref.shape[0]
    per_group = SUBLANES // T
    row = lax.broadcasted_iota(jnp.int32, (TM, 1), 0)
    pos = row % T
    q = _group(proj_ref, 0, rows)
    v = _group(proj_ref, 2, rows)
    g, kk = _forget_gates(_group(proj_ref, 1, rows), lb)

    G = g
    for d in range(1, T):
        G = G + jnp.where(pos >= d, _shift_rows(g, d), 0.0)
    o = jnp.sum(q * kk, axis=-1, keepdims=True) * v
    for d in range(1, T):
        decay = jnp.exp(jnp.minimum(G - _shift_rows(G, d), 0.0))
        a = jnp.sum(jnp.where(pos >= d, q * _shift_rows(kk, d) * decay, 0.0), axis=-1, keepdims=True)
        o = o + a * _shift_rows(v, d)
    GL = G
    for d in range(1, T):
        GL = jnp.where(pos == T - 1 - d, _shift_rows(G, -d), GL)
    qt_ref[...] = q * jnp.exp(G)
    kh_ref[...] = kk * jnp.exp(GL - G)
    dec_ref[...] = jnp.exp(GL)
    oi_ref[...] = v

    row8 = lax.broadcasted_iota(jnp.int32, (SUBLANES, 1), 0)

    def group_body(p, carry):
        r = pl.multiple_of(p * SUBLANES, SUBLANES)
        q8 = qt_ref[pl.ds(r, SUBLANES), :].astype(_BF16)
        k8 = kh_ref[pl.ds(r, SUBLANES), :]
        v8 = oi_ref[pl.ds(r, SUBLANES), :].astype(_BF16)
        d8 = dec_ref[pl.ds(r, SUBLANES), :]
        o8 = jnp.zeros((SUBLANES, LANES), _F32)
        for s in range(per_group):
            b = p * per_group + s
            mine = (row8 >= s * T) & (row8 < (s + 1) * T)
            S0 = sin_ref[b]
            o8 = jnp.where(mine, jnp.dot(q8, S0.astype(_BF16), preferred_element_type=_F32), o8)
            upd = pl.dot(jnp.where(mine, k8, 0.0).astype(_BF16), v8, trans_a=True)
            decay = jnp.broadcast_to(d8[s * T:s * T + 1], (LANES, LANES)).T
            sout_ref[b] = decay * S0 + upd
        oi_ref[pl.ds(r, SUBLANES), :] = o8
        return carry

    lax.fori_loop(0, TM // SUBLANES, group_body, 0, unroll=SAMPLE_UNROLL)

    o = o + oi_ref[...]
    o_ref[rows, :] = (o * _rms_scale(o) * nw * _silu(_group(proj_ref, 3, rows))).astype(o_ref.dtype)

    u = _group(proj_ref, 5, rows) * _group(proj_ref, 6, rows)
    e = cpad_ref[...]
    um1 = jnp.where(pos >= 1, _shift_rows(u, 1), _shift_rows(e, -1))
    um2 = jnp.where(pos >= 2, _shift_rows(u, 2), e)
    yb_ref[rows, :] = _conv_gate(_group(proj_ref, 4, rows), _group(proj_ref, 7, rows), u, um1, um2,
                                 cw).astype(yb_ref.dtype)
    u_ref[rows, :] = u


def _first_norm_kernel(xp_ref, xs_ref, xm_ref, w_ref, hn_ref, hnm_ref, *, n_prompt_tiles):
    i = pl.program_id(0)
    x = jnp.where(i < n_prompt_tiles, xp_ref[...], xs_ref[...])
    hn_ref[...] = (x * _rms_scale(x) * w_ref[...]).astype(_BF16)

    @pl.when(i == 0)
    def _():
        xm = xm_ref[...]
        hnm_ref[...] = (xm * _rms_scale(xm) * w_ref[...]).astype(_BF16)


def _first_norm(xp, xs, xm, pre_w):
    Rp, D = xp.shape
    n_p = Rp // ROW_TILE
    n_meta = xm.shape[0]
    return pl.pallas_call(
        functools.partial(_first_norm_kernel, n_prompt_tiles=n_p), grid=(n_p + 1,),
        in_specs=[pl.BlockSpec((ROW_TILE, D), lambda i: (jnp.minimum(i, n_p - 1), 0)),
                  pl.BlockSpec((ROW_TILE, D), lambda i: (0, 0)),
                  pl.BlockSpec((n_meta, D), lambda i: (0, 0)),
                  pl.BlockSpec((None, 1, D), lambda i: (0, 0, 0))],
        out_specs=[pl.BlockSpec((ROW_TILE, D), lambda i: (i, 0)),
                   pl.BlockSpec((n_meta, D), lambda i: (0, 0))],
        out_shape=[jax.ShapeDtypeStruct((Rp + ROW_TILE, D), _BF16),
                   jax.ShapeDtypeStruct((n_meta, D), _BF16)],
        compiler_params=pltpu.CompilerParams(dimension_semantics=("arbitrary",),
                                             vmem_limit_bytes=VMEM_LIMIT),
        name="first_norm",
    )(xp, xs, xm, pre_w)


def _mixer_kernel(hnm_ref, hn_ref, *refs, layer, n_prompt_tiles, tiles_per_seq, seq_len, aliased):
    slabs = refs[:N_GROUPS]
    refs = refs[N_GROUPS:]
    if aliased:
        refs = refs[1:]
    (lbp_ref, nw_ref, cw_ref, sin_ref, cpad_ref,
     o_ref, yb_ref, om_ref, ybm_ref, sp_ref, cp_ref, ss_ref, us_ref,
     w_scr, proj_ref, next_ref, projm_ref, st_ref, cb_ref, st0_ref, cb0_ref, g_ref,
     qt_ref, kh_ref, dec_ref, oi_ref) = refs
    s = pl.program_id(1)
    n_meta = hnm_ref.shape[0]
    lb = _lower_bound(lbp_ref[...], layer)
    nw = nw_ref[...]
    cw = cw_ref[...]

    @pl.when(s == 0)
    def _():
        _cast_column_groups(slabs, w_scr)
        projm_ref[...] = jnp.dot(hnm_ref[...], w_scr[...], preferred_element_type=_F32)
        st_ref[...] = jnp.zeros_like(st_ref)
        cb_ref[...] = jnp.zeros_like(cb_ref)
        _chunked_head_mixer(projm_ref, lb, nw, cw, st_ref, cb_ref, g_ref, n_meta, om_ref, ybm_ref)
        st0_ref[...] = st_ref[...]
        cb0_ref[...] = cb_ref[...]
        proj_ref[...] = jnp.dot(hn_ref[...], w_scr[...], preferred_element_type=_F32)

    @pl.when((s >= 1) & (s <= n_prompt_tiles))
    def _():
        @pl.when((s - 1) % tiles_per_seq == 0)
        def _():
            st_ref[...] = st0_ref[...]
            cb_ref[...] = cb0_ref[...]

        n_cols = N_GROUPS * LANES

        def project_next_piece(c, n_chunks):
            cols = slice(c * n_cols // n_chunks, (c + 1) * n_cols // n_chunks)
            next_ref[:, cols] = jnp.dot(hn_ref[...], w_scr[:, cols], preferred_element_type=_F32)

        _chunked_head_mixer(proj_ref, lb, nw, cw, st_ref, cb_ref, g_ref, PROMPT_CHUNK, o_ref, yb_ref,
                            before_chunk=project_next_piece)
        cp_ref[...] = cb_ref[...]
        sp_ref[...] = st_ref[...].T
        proj_ref[...] = next_ref[...]

    @pl.when(s > n_prompt_tiles)
    def _():
        start = pl.multiple_of((s - n_prompt_tiles - 1) * SAMPLE_PIECE, SAMPLE_PIECE)
        rows = pl.ds(start, SAMPLE_PIECE)
        _sample_head_mixer(proj_ref, rows, lb, nw, cw, sin_ref, ss_ref, cpad_ref, o_ref, yb_ref, us_ref,
                           qt_ref, kh_ref, dec_ref, oi_ref, seq_len)


def _mixer(hnm, hn, w_in, lb_param, nw, conv_w, state, cpad, prev_states, *, layer, tiles_per_seq, seq_len):
    R, D = hn.shape
    n_meta = hnm.shape[0]
    W = w_in.shape[2] // N_GROUPS
    H = W // LANES
    depth = lb_param.shape[0]
    n_p = R // ROW_TILE - 1
    n_pieces = ROW_TILE // SAMPLE_PIECE
    seq_per_piece = SAMPLE_PIECE // seq_len
    n_steps = 1 + n_p + n_pieces
    aliased = prev_states is not None

    def tile_of(s):
        return jnp.clip(s, 0, n_p)

    def out_tile_of(s):
        return jnp.clip(s - 1, 0, n_p)

    def prompt_tile_of(s):
        return jnp.clip(s - 1, 0, n_p - 1)

    def piece_of(s):
        return jnp.clip(s - 1 - n_p, 0, n_pieces - 1)

    kern = functools.partial(_mixer_kernel, layer=layer, n_prompt_tiles=n_p, tiles_per_seq=tiles_per_seq,
                             seq_len=seq_len, aliased=aliased)
    slab_specs = [pl.BlockSpec((None, D, LANES), lambda j, s, g=g: (layer, 0, g * H + j))
                  for g in range(N_GROUPS)]
    in_specs = (
        [pl.BlockSpec((n_meta, D), lambda j, s: (0, 0)),
         pl.BlockSpec((ROW_TILE, D), lambda j, s: (tile_of(s), 0))]
        + slab_specs
        + ([pl.BlockSpec(memory_space=pl.ANY)] if aliased else [])
        + [pl.BlockSpec((depth, LANES), lambda j, s: (0, j)),
           pl.BlockSpec((None, 1, LANES), lambda j, s: (layer, 0, 0)),
           pl.BlockSpec((None, CONV_W, LANES), lambda j, s: (layer, 0, j)),
           pl.BlockSpec((None, seq_per_piece, None, LANES, LANES), lambda j, s: (layer, piece_of(s), j, 0, 0)),
           pl.BlockSpec((None, SAMPLE_PIECE, LANES), lambda j, s: (layer, piece_of(s), j))])
    out_specs = [
        pl.BlockSpec((ROW_TILE, LANES), lambda j, s: (out_tile_of(s), j)),
        pl.BlockSpec((ROW_TILE, LANES), lambda j, s: (out_tile_of(s), j)),
        pl.BlockSpec((n_meta, LANES), lambda j, s: (0, j)),
        pl.BlockSpec((n_meta, LANES), lambda j, s: (0, j)),
        pl.BlockSpec((None, None, LANES, LANES), lambda j, s: (prompt_tile_of(s), j, 0, 0)),
        pl.BlockSpec((None, SUBLANES, LANES), lambda j, s: (prompt_tile_of(s), 0, j)),
        pl.BlockSpec((None, seq_per_piece, None, LANES, LANES), lambda j, s: (layer, piece_of(s), j, 0, 0)),
        pl.BlockSpec((ROW_TILE, LANES), lambda j, s: (0, j)),
    ]
    out_shape = [
        jax.ShapeDtypeStruct((R, W), _BF16),
        jax.ShapeDtypeStruct((R, W), _BF16),
        jax.ShapeDtypeStruct((n_meta, W), _BF16),
        jax.ShapeDtypeStruct((n_meta, W), _BF16),
        jax.ShapeDtypeStruct((n_p, H, LANES, LANES), _F32),
        jax.ShapeDtypeStruct((n_p, SUBLANES, W), _F32),
        jax.ShapeDtypeStruct(state.shape, _F32),
        jax.ShapeDtypeStruct((ROW_TILE, W), _F32),
    ]
    scratch = [
        pltpu.VMEM((D, N_GROUPS * LANES), _BF16),
        pltpu.VMEM((ROW_TILE, N_GROUPS * LANES), _F32),
        pltpu.VMEM((ROW_TILE, N_GROUPS * LANES), _F32),
        pltpu.VMEM((n_meta, N_GROUPS * LANES), _F32),
        pltpu.VMEM((LANES, LANES), _F32),
        pltpu.VMEM((SUBLANES, LANES), _F32),
        pltpu.VMEM((LANES, LANES), _F32),
        pltpu.VMEM((SUBLANES, LANES), _F32),
        pltpu.VMEM((PROMPT_CHUNK, LANES), _F32),
    ] + [pltpu.VMEM((SAMPLE_PIECE, LANES), _F32)] * 4
    args = [hnm, hn] + [w_in] * N_GROUPS + ([prev_states] if aliased else []) + [lb_param, nw, conv_w, state, cpad]
    return pl.pallas_call(
        kern, grid=(H, n_steps), in_specs=in_specs, out_specs=out_specs, out_shape=out_shape,
        scratch_shapes=scratch,
        input_output_aliases={2 + N_GROUPS: 6} if aliased else {},
        compiler_params=pltpu.CompilerParams(dimension_semantics=("arbitrary", "arbitrary"),
                                             vmem_limit_bytes=VMEM_LIMIT),
        name="mixer",
    )(*args)


def _out_kernel(*refs, n_prompt_tiles, first, last):
    refs = list(refs)
    if first:
        hp_ref, hs_ref = refs[:2]
        refs = refs[2:]
    else:
        h_ref = refs.pop(0)
    hm_ref, o_ref, yb_ref, om_ref, ybm_ref, wout_ref, post_ref = refs[:7]
    refs = refs[7:]
    if last:
        yp_ref, ys_ref, wo_scr = refs
    else:
        pre_ref, hout_ref, hn_ref, hmout_ref, hnm_ref, wo_scr = refs
    i = pl.program_id(0)
    W = o_ref.shape[1]

    def mixed(h, o, yb):
        mix = (jnp.dot(o, wo_scr[0:W], preferred_element_type=_F32)
               + jnp.dot(yb, wo_scr[W:2 * W], preferred_element_type=_F32))
        return h + mix * _rms_scale(mix) * post_ref[...]

    @pl.when(i == 0)
    def _():
        wo_scr[...] = wout_ref[...].astype(_BF16)
        if not last:
            hm = mixed(hm_ref[...], om_ref[...], ybm_ref[...])
            hmout_ref[...] = hm
            hnm_ref[...] = (hm * _rms_scale(hm) * pre_ref[...]).astype(_BF16)

    if first:
        h_in = jnp.where(i < n_prompt_tiles, hp_ref[...], hs_ref[...])
    else:
        h_in = h_ref[...]
    h = mixed(h_in, o_ref[...], yb_ref[...])
    if last:
        @pl.when(i < n_prompt_tiles)
        def _():
            yp_ref[...] = h

        @pl.when(i >= n_prompt_tiles)
        def _():
            ys_ref[...] = h
    else:
        hout_ref[...] = h
        hn_ref[...] = (h * _rms_scale(h) * pre_ref[...]).astype(_BF16)


def _out(h_args, hm, o, yb, om, ybm, w_out, post_w, pre_w, *, layer, first, last):
    R, W = o.shape
    D = w_out.shape[2]
    n_meta = om.shape[0]
    tile = OUT_TILE
    n_p = (R - ROW_TILE) // tile
    n_s = ROW_TILE // tile
    kern = functools.partial(_out_kernel, n_prompt_tiles=n_p, first=first, last=last)
    tile_spec = pl.BlockSpec((tile, D), lambda i: (i, 0))
    prompt_spec = pl.BlockSpec((tile, D), lambda i: (jnp.minimum(i, n_p - 1), 0))
    sample_spec = pl.BlockSpec((tile, D), lambda i: (jnp.maximum(i - n_p, 0), 0))
    meta_spec = pl.BlockSpec((n_meta, D), lambda i: (0, 0))
    in_specs = ([prompt_spec, sample_spec] if first else [tile_spec]) + [
        meta_spec,
        pl.BlockSpec((tile, W), lambda i: (i, 0)),
        pl.BlockSpec((tile, W), lambda i: (i, 0)),
        pl.BlockSpec((n_meta, W), lambda i: (0, 0)),
        pl.BlockSpec((n_meta, W), lambda i: (0, 0)),
        pl.BlockSpec((None, 2 * W, D), lambda i: (layer, 0, 0), pipeline_mode=pl.Buffered(1)),
        pl.BlockSpec((None, 1, D), lambda i: (layer, 0, 0)),
    ]
    args = list(h_args) + [hm, o, yb, om, ybm, w_out, post_w]
    if last:
        out_specs = [prompt_spec, sample_spec]
        out_shape = [jax.ShapeDtypeStruct((n_p * tile, D), _F32),
                     jax.ShapeDtypeStruct((ROW_TILE, D), _F32)]
    else:
        in_specs.append(pl.BlockSpec((None, 1, D), lambda i: (layer + 1, 0, 0)))
        args.append(pre_w)
        out_specs = [tile_spec, tile_spec, meta_spec, meta_spec]
        out_shape = [jax.ShapeDtypeStruct((R, D), _F32), jax.ShapeDtypeStruct((R, D), _BF16),
                     jax.ShapeDtypeStruct((n_meta, D), _F32), jax.ShapeDtypeStruct((n_meta, D), _BF16)]
    return pl.pallas_call(
        kern, grid=(n_p + n_s,), in_specs=in_specs, out_specs=out_specs, out_shape=out_shape,
        scratch_shapes=[pltpu.VMEM((2 * W, D), _BF16)],
        compiler_params=pltpu.CompilerParams(dimension_semantics=("arbitrary",),
                                             vmem_limit_bytes=VMEM_LIMIT),
        name="out_proj",
    )(*args)


def kernel(x_prompt, x_sample, state_hgrn, state_conv, meta_tokens, w_in, conv_w, lb_param,
           hgrn_norm_w, w_out, pre_norm_w, post_norm_w):
    B, T, D = x_prompt.shape
    DB, DT, _ = x_sample.shape
    depth = w_in.shape[0]
    n_meta = meta_tokens.shape[0]
    W = D // 2
    assert W % LANES == 0 and SUBLANES % DT == 0 and n_meta % (2 * SUBLANES) == 0
    assert T % ROW_TILE == 0 and DB * DT == ROW_TILE and n_meta <= PROMPT_CHUNK
    assert w_in.shape == (depth, D, N_GROUPS * W) and state_conv.shape[2] == CONV_W - 1

    nw = hgrn_norm_w.reshape(depth, 1, LANES)
    pre = pre_norm_w.reshape(depth, 1, D)
    post = post_norm_w.reshape(depth, 1, D)
    xp = x_prompt.reshape(B * T, D)
    xs = x_sample.reshape(DB * DT, D)
    cpad = jnp.concatenate(
        [state_conv, jnp.zeros((depth, DB, DT - (CONV_W - 1), W), _F32)], axis=2
    ).reshape(depth, DB * DT, W)
    tiles_per_seq = T // ROW_TILE
    last_tiles = slice(tiles_per_seq - 1, None, tiles_per_seq)

    hn, hnm = _first_norm(xp, xs, meta_tokens, pre)
    h_args, hm = (xp, xs), meta_tokens
    s_sample = None
    s_list, c_list, u_list = [], [], []
    for layer in range(depth):
        o, yb, om, ybm, s_p, c_p, s_sample, u_s = _mixer(
            hnm, hn, w_in, lb_param, nw, conv_w, state_hgrn, cpad, s_sample,
            layer=layer, tiles_per_seq=tiles_per_seq, seq_len=DT)
        s_list.append(s_p[last_tiles])
        c_list.append(c_p[last_tiles, SUBLANES - (CONV_W - 1):])
        u_list.append(u_s.reshape(DB, DT, W)[:, DT - (CONV_W - 1):])
        last = layer + 1 == depth
        outs = _out(h_args, hm, o, yb, om, ybm, w_out, post, pre, layer=layer, first=layer == 0, last=last)
        if not last:
            h_all, hn, hm, hnm = outs
            h_args = (h_all,)
    y_prompt = outs[0].reshape(B, T, D)
    y_sample = outs[1].reshape(DB, DT, D)
    return (y_prompt, y_sample, jnp.stack(s_list), jnp.stack(c_list), s_sample, jnp.stack(u_list))
```

```python
import functools

import jax
import jax.numpy as jnp
from jax import lax
from jax.experimental import pallas as pl
from jax.experimental.pallas import tpu as pltpu

LANES = 128
SUBLANES = 8
N_GROUPS = 8
CONV_W = 3
EPS = 1e-6
F_FLOOR = 1e-30
LOG2_E = 1.4426950408889634
ROW_TILE = 512
PROMPT_CHUNK = 128
SAMPLE_PIECE = 128
SAMPLE_UNROLL = 4
VMEM_LIMIT = 56 * 1024 * 1024

_F32 = jnp.float32
_BF16 = jnp.bfloat16
_NT = (((1,), (1,)), ((), ()))
_O_COLS = slice(0, LANES)
_YB_COLS = slice(LANES, 2 * LANES)


def _sigmoid(x):
    return 1.0 / (1.0 + jnp.exp(-x))


def _silu(x):
    return x * _sigmoid(x)


def _rms_scale(x):
    return lax.rsqrt(jnp.mean(x * x, axis=-1, keepdims=True) + EPS)


def _lower_bound(lbp, layer):
    e = jnp.exp(lbp - jnp.max(lbp, axis=0, keepdims=True))
    w = e / jnp.sum(e, axis=0, keepdims=True)
    r = lax.broadcasted_iota(jnp.int32, w.shape, 0)
    return jnp.sum(jnp.where((r >= 1) & (r <= layer), w, 0.0), axis=0, keepdims=True)


def _forget_gates(fz, lb):
    one_m = 1.0 - lb
    f = lb + one_m * _sigmoid(fz)
    return jnp.log(jnp.maximum(f, F_FLOOR)), one_m * _sigmoid(-fz)


def _shift_rows(x, s):
    return pltpu.roll(x, s % x.shape[0], 0)


def _cumsum_rows(x, row):
    s = 1
    while s < x.shape[0]:
        x = x + jnp.where(row >= s, _shift_rows(x, s), 0.0)
        s *= 2
    return x


def _anchor_rows(g_ref, G, n, row):
    C = G.shape[0]
    half = n // 2
    if n == 2:
        return jnp.where((row & 1) == 1, _shift_rows(G, 1), G)
    if n == 4:
        pos = row & 3
        return jnp.where(pos == 0, _shift_rows(G, -1),
                         jnp.where(pos == 1, G,
                                   jnp.where(pos == 2, _shift_rows(G, 1), _shift_rows(G, 2))))
    parts = []
    for a in range(C // n):
        r = g_ref[pl.ds(a * n + half - 1, 1), :]
        parts.append(jnp.broadcast_to(r, (n, LANES)))
    return parts[0] if len(parts) == 1 else jnp.concatenate(parts, axis=0)


def _intra_chunk_scores(q, kk, G2, g_ref, row, col):
    C = q.shape[0]
    A = jnp.where(row == col, jnp.sum(q * kk, axis=-1, keepdims=True), 0.0)
    q16 = q.astype(_BF16)
    k16 = kk.astype(_BF16)
    n = 2
    while n <= C:
        half = n // 2
        E = jnp.exp2(-jnp.abs(G2 - _anchor_rows(g_ref, G2, n, row))).astype(_BF16)
        P = lax.dot_general(q16 * E, k16 * E, _NT, preferred_element_type=_F32)
        shift = n.bit_length() - 1
        mask = ((row >> shift) == (col >> shift)) & ((row & (n - 1)) >= half) & ((col & (n - 1)) < half)
        A = jnp.where(mask, P, A)
        n *= 2
    return A


def _group(ref, g, rows=slice(None)):
    return ref[rows, g * LANES:(g + 1) * LANES]


def _conv_gate(bg, zb, u, um1, um2, cw):
    c = cw[0:1] * um2 + cw[1:2] * um1 + cw[2:3] * u
    return bg * c * _silu(zb)


def _chunked_head_mixer(proj_ref, lb, nw, cw, st_ref, cb_ref, g_ref, chunk, oy_ref, before_chunk=None):
    TM = proj_ref.shape[0]
    C = chunk
    row = lax.broadcasted_iota(jnp.int32, (C, 1), 0)
    col = lax.broadcasted_iota(jnp.int32, (1, C), 1)
    for c in range(TM // C):
        if before_chunk is not None:
            before_chunk(c, TM // C)
        rows = slice(c * C, (c + 1) * C)
        q = _group(proj_ref, 0, rows)
        v16 = _group(proj_ref, 2, rows).astype(_BF16)
        g, kk = _forget_gates(_group(proj_ref, 1, rows), lb)
        G2 = _cumsum_rows(g * LOG2_E, row)
        g_ref[0:C] = G2
        A = _intra_chunk_scores(q, kk, G2, g_ref, row, col)
        St = st_ref[...]
        o = jnp.dot(A.astype(_BF16), v16, preferred_element_type=_F32)
        o = o + lax.dot_general((q * jnp.exp2(G2)).astype(_BF16), St.astype(_BF16), _NT,
                                preferred_element_type=_F32)
        GL2 = G2[C - 1:C]
        khat = (kk * jnp.exp2(GL2 - G2)).astype(_BF16)
        st_ref[...] = St * jnp.exp2(GL2) + pl.dot(v16, khat, trans_a=True)
        o = o * _rms_scale(o) * nw * _silu(_group(proj_ref, 3, rows))
        oy_ref[rows, _O_COLS] = o.astype(oy_ref.dtype)

    u = _group(proj_ref, 5) * _group(proj_ref, 6)
    prev = cb_ref[...]
    trow = lax.broadcasted_iota(jnp.int32, (TM, 1), 0)
    um1 = jnp.where(trow >= 1, _shift_rows(u, 1), prev[7:8])
    um2 = jnp.where(trow >= 2, _shift_rows(u, 2), jnp.where(trow == 0, prev[6:7], prev[7:8]))
    oy_ref[:, _YB_COLS] = _conv_gate(_group(proj_ref, 4), _group(proj_ref, 7), u, um1, um2,
                                     cw).astype(oy_ref.dtype)
    cb_ref[...] = u[TM - SUBLANES:TM]


def _sample_head_mixer(proj_ref, rows, lb, nw, cw, sin_ref, sout_ref, cpad_ref, oy_ref, u_ref,
                       qt_ref, kh_ref, dec_ref, oi_ref, seq_len):
    T = seq_len
    TM = qt_ref.shape[0]
    per_group = SUBLANES // T
    row = lax.broadcasted_iota(jnp.int32, (TM, 1), 0)
    pos = row % T
    q = _group(proj_ref, 0, rows)
    v = _group(proj_ref, 2, rows)
    g, kk = _forget_gates(_group(proj_ref, 1, rows), lb)

    G = g
    for d in range(1, T):
        G = G + jnp.where(pos >= d, _shift_rows(g, d), 0.0)
    o = jnp.sum(q * kk, axis=-1, keepdims=True) * v
    for d in range(1, T):
        decay = jnp.exp(jnp.minimum(G - _shift_rows(G, d), 0.0))
        a = jnp.sum(jnp.where(pos >= d, q * _shift_rows(kk, d) * decay, 0.0), axis=-1, keepdims=True)
        o = o + a * _shift_rows(v, d)
    GL = G
    for d in range(1, T):
        GL = jnp.where(pos == T - 1 - d, _shift_rows(G, -d), GL)
    qt_ref[...] = q * jnp.exp(G)
    kh_ref[...] = kk * jnp.exp(GL - G)
    dec_ref[...] = jnp.exp(GL)
    oi_ref[...] = v

    row8 = lax.broadcasted_iota(jnp.int32, (SUBLANES, 1), 0)

    def group_body(p, carry):
        r = pl.multiple_of(p * SUBLANES, SUBLANES)
        q8 = qt_ref[pl.ds(r, SUBLANES), :].astype(_BF16)
        k8 = kh_ref[pl.ds(r, SUBLANES), :]
        v8 = oi_ref[pl.ds(r, SUBLANES), :].astype(_BF16)
        d8 = dec_ref[pl.ds(r, SUBLANES), :]
        o8 = jnp.zeros((SUBLANES, LANES), _F32)
        for s in range(per_group):
            b = p * per_group + s
            mine = (row8 >= s * T) & (row8 < (s + 1) * T)
            S0 = sin_ref[b]
            o8 = jnp.where(mine, jnp.dot(q8, S0.astype(_BF16), preferred_element_type=_F32), o8)
            upd = pl.dot(jnp.where(mine, k8, 0.0).astype(_BF16), v8, trans_a=True)
            decay = jnp.broadcast_to(d8[s * T:s * T + 1], (LANES, LANES)).T
            sout_ref[b] = decay * S0 + upd
        oi_ref[pl.ds(r, SUBLANES), :] = o8
        return carry

    lax.fori_loop(0, TM // SUBLANES, group_body, 0, unroll=SAMPLE_UNROLL)

    o = o + oi_ref[...]
    o = o * _rms_scale(o) * nw * _silu(_group(proj_ref, 3, rows))
    oy_ref[rows, _O_COLS] = o.astype(oy_ref.dtype)

    u = _group(proj_ref, 5, rows) * _group(proj_ref, 6, rows)
    e = cpad_ref[...]
    um1 = jnp.where(pos >= 1, _shift_rows(u, 1), _shift_rows(e, -1))
    um2 = jnp.where(pos >= 2, _shift_rows(u, 2), e)
    oy_ref[rows, _YB_COLS] = _conv_gate(_group(proj_ref, 4, rows), _group(proj_ref, 7, rows), u, um1, um2,
                                        cw).astype(oy_ref.dtype)
    u_ref[rows, :] = u


def _first_norm_kernel(xp_ref, xs_ref, xm_ref, w_ref, hn_ref, hnm_ref, *, n_prompt_tiles):
    i = pl.program_id(0)
    x = jnp.where(i < n_prompt_tiles, xp_ref[...], xs_ref[...])
    hn_ref[...] = (x * _rms_scale(x) * w_ref[...]).astype(_BF16)

    @pl.when(i == 0)
    def _():
        xm = xm_ref[...]
        hnm_ref[...] = (xm * _rms_scale(xm) * w_ref[...]).astype(_BF16)


def _first_norm(xp, xs, xm, pre_w):
    Rp, D = xp.shape
    n_p = Rp // ROW_TILE
    n_meta = xm.shape[0]
    return pl.pallas_call(
        functools.partial(_first_norm_kernel, n_prompt_tiles=n_p), grid=(n_p + 1,),
        in_specs=[pl.BlockSpec((ROW_TILE, D), lambda i: (jnp.minimum(i, n_p - 1), 0)),
                  pl.BlockSpec((ROW_TILE, D), lambda i: (0, 0)),
                  pl.BlockSpec((n_meta, D), lambda i: (0, 0)),
                  pl.BlockSpec((None, 1, D), lambda i: (0, 0, 0))],
        out_specs=[pl.BlockSpec((ROW_TILE, D), lambda i: (i, 0)),
                   pl.BlockSpec((n_meta, D), lambda i: (0, 0))],
        out_shape=[jax.ShapeDtypeStruct((Rp + ROW_TILE, D), _BF16),
                   jax.ShapeDtypeStruct((n_meta, D), _BF16)],
        compiler_params=pltpu.CompilerParams(dimension_semantics=("arbitrary",),
                                             vmem_limit_bytes=VMEM_LIMIT),
        name="first_norm",
    )(xp, xs, xm, pre_w)


def _mixer_kernel(hnm_ref, hn_ref, win_ref, woa_ref, wob_ref, *refs,
                  layer, depth, n_prompt_tiles, tiles_per_seq, seq_len, aliased):
    if aliased:
        refs = refs[1:]
    (prm_ref, sin_ref, cpad_ref,
     oy_ref, oym_ref, sp_ref, cp_ref, ss_ref, us_ref, wo16_ref,
     w_scr, proj_ref, next_ref, projm_ref, st_ref, cb_ref, st0_ref, cb0_ref, g_ref,
     qt_ref, kh_ref, dec_ref, oi_ref) = refs
    s = pl.program_id(1)
    n_meta, D = hnm_ref.shape
    lb = _lower_bound(prm_ref[0:depth], layer)
    cw = prm_ref[depth:depth + CONV_W]
    nw = prm_ref[depth + CONV_W:depth + CONV_W + 1]

    @pl.when(s == 0)
    def _():
        for g in range(N_GROUPS):
            w_scr[:, g * LANES:(g + 1) * LANES] = win_ref[pl.ds(g, D, stride=N_GROUPS), :].astype(_BF16)
        wo16_ref[_O_COLS.start:_O_COLS.stop] = woa_ref[...].astype(_BF16)
        wo16_ref[_YB_COLS.start:_YB_COLS.stop] = wob_ref[...].astype(_BF16)
        projm_ref[...] = jnp.dot(hnm_ref[...], w_scr[...], preferred_element_type=_F32)
        st_ref[...] = jnp.zeros_like(st_ref)
        cb_ref[...] = jnp.zeros_like(cb_ref)
        _chunked_head_mixer(projm_ref, lb, nw, cw, st_ref, cb_ref, g_ref, n_meta, oym_ref)
        st0_ref[...] = st_ref[...]
        cb0_ref[...] = cb_ref[...]
        proj_ref[...] = jnp.dot(hn_ref[...], w_scr[...], preferred_element_type=_F32)

    @pl.when((s >= 1) & (s <= n_prompt_tiles))
    def _():
        @pl.when((s - 1) % tiles_per_seq == 0)
        def _():
            st_ref[...] = st0_ref[...]
            cb_ref[...] = cb0_ref[...]

        n_cols = N_GROUPS * LANES

        def project_next_piece(c, n_chunks):
            cols = slice(c * n_cols // n_chunks, (c + 1) * n_cols // n_chunks)
            next_ref[:, cols] = jnp.dot(hn_ref[...], w_scr[:, cols], preferred_element_type=_F32)

        _chunked_head_mixer(proj_ref, lb, nw, cw, st_ref, cb_ref, g_ref, PROMPT_CHUNK, oy_ref,
                            before_chunk=project_next_piece)
        cp_ref[...] = cb_ref[...]
        sp_ref[...] = st_ref[...].T
        proj_ref[...] = next_ref[...]

    @pl.when(s > n_prompt_tiles)
    def _():
        start = pl.multiple_of((s - n_prompt_tiles - 1) * SAMPLE_PIECE, SAMPLE_PIECE)
        rows = pl.ds(start, SAMPLE_PIECE)
        _sample_head_mixer(proj_ref, rows, lb, nw, cw, sin_ref, ss_ref, cpad_ref, oy_ref, us_ref,
                           qt_ref, kh_ref, dec_ref, oi_ref, seq_len)


def _mixer(hnm, hn, w_in, w_out, head_params, state, cpad, prev_states, *, layer, tiles_per_seq, seq_len):
    R, D = hn.shape
    n_meta = hnm.shape[0]
    depth, _, W = w_in.shape
    H = W // LANES
    n_p = R // ROW_TILE - 1
    n_pieces = ROW_TILE // SAMPLE_PIECE
    seq_per_piece = SAMPLE_PIECE // seq_len
    n_steps = 1 + n_p + n_pieces
    aliased = prev_states is not None

    def tile_of(s):
        return jnp.clip(s, 0, n_p)

    def out_tile_of(s):
        return jnp.clip(s - 1, 0, n_p)

    def prompt_tile_of(s):
        return jnp.clip(s - 1, 0, n_p - 1)

    def piece_of(s):
        return jnp.clip(s - 1 - n_p, 0, n_pieces - 1)

    kern = functools.partial(_mixer_kernel, layer=layer, depth=depth, n_prompt_tiles=n_p,
                             tiles_per_seq=tiles_per_seq, seq_len=seq_len, aliased=aliased)
    in_specs = (
        [pl.BlockSpec((n_meta, D), lambda j, s: (0, 0)),
         pl.BlockSpec((ROW_TILE, D), lambda j, s: (tile_of(s), 0)),
         pl.BlockSpec((None, D * N_GROUPS, LANES), lambda j, s: (layer, 0, j)),
         pl.BlockSpec((None, LANES, D), lambda j, s: (layer, j, 0)),
         pl.BlockSpec((None, LANES, D), lambda j, s: (layer, H + j, 0))]
        + ([pl.BlockSpec(memory_space=pl.ANY)] if aliased else [])
        + [pl.BlockSpec((None, head_params.shape[1], LANES), lambda j, s: (layer, 0, j)),
           pl.BlockSpec((None, seq_per_piece, None, LANES, LANES), lambda j, s: (layer, piece_of(s), j, 0, 0)),
           pl.BlockSpec((None, SAMPLE_PIECE, LANES), lambda j, s: (layer, piece_of(s), j))])
    out_specs = [
        pl.BlockSpec((ROW_TILE, 2 * LANES), lambda j, s: (out_tile_of(s), j)),
        pl.BlockSpec((n_meta, 2 * LANES), lambda j, s: (0, j)),
        pl.BlockSpec((None, None, LANES, LANES), lambda j, s: (prompt_tile_of(s), j, 0, 0)),
        pl.BlockSpec((None, SUBLANES, LANES), lambda j, s: (prompt_tile_of(s), 0, j)),
        pl.BlockSpec((None, seq_per_piece, None, LANES, LANES), lambda j, s: (layer, piece_of(s), j, 0, 0)),
        pl.BlockSpec((ROW_TILE, LANES), lambda j, s: (0, j)),
        pl.BlockSpec((2 * LANES, D), lambda j, s: (j, 0)),
    ]
    out_shape = [
        jax.ShapeDtypeStruct((R, 2 * W), _BF16),
        jax.ShapeDtypeStruct((n_meta, 2 * W), _BF16),
        jax.ShapeDtypeStruct((n_p, H, LANES, LANES), _F32),
        jax.ShapeDtypeStruct((n_p, SUBLANES, W), _F32),
        jax.ShapeDtypeStruct(state.shape, _F32),
        jax.ShapeDtypeStruct((ROW_TILE, W), _F32),
        jax.ShapeDtypeStruct((2 * W, D), _BF16),
    ]
    scratch = [
        pltpu.VMEM((D, N_GROUPS * LANES), _BF16),
        pltpu.VMEM((ROW_TILE, N_GROUPS * LANES), _F32),
        pltpu.VMEM((ROW_TILE, N_GROUPS * LANES), _F32),
        pltpu.VMEM((n_meta, N_GROUPS * LANES), _F32),
        pltpu.VMEM((LANES, LANES), _F32),
        pltpu.VMEM((SUBLANES, LANES), _F32),
        pltpu.VMEM((LANES, LANES), _F32),
        pltpu.VMEM((SUBLANES, LANES), _F32),
        pltpu.VMEM((PROMPT_CHUNK, LANES), _F32),
    ] + [pltpu.VMEM((SAMPLE_PIECE, LANES), _F32)] * 4
    args = [hnm, hn, w_in, w_out, w_out] + ([prev_states] if aliased else []) + [head_params, state, cpad]
    return pl.pallas_call(
        kern, grid=(H, n_steps), in_specs=in_specs, out_specs=out_specs, out_shape=out_shape,
        scratch_shapes=scratch,
        input_output_aliases={5: 4} if aliased else {},
        compiler_params=pltpu.CompilerParams(dimension_semantics=("arbitrary", "arbitrary"),
                                             vmem_limit_bytes=VMEM_LIMIT),
        name="mixer",
    )(*args)


def _out_kernel(*refs, n_prompt_tiles, first, last):
    refs = list(refs)
    if first:
        hp_ref, hs_ref = refs[:2]
        refs = refs[2:]
    else:
        h_ref = refs.pop(0)
    hm_ref, oy_ref, oym_ref, wo_ref, post_ref = refs[:5]
    refs = refs[5:]
    if last:
        yp_ref, ys_ref = refs
    else:
        pre_ref, hout_ref, hn_ref, hmout_ref, hnm_ref = refs
    i = pl.program_id(0)

    def mixed(h, oy):
        mix = jnp.dot(oy, wo_ref[...], preferred_element_type=_F32)
        return h + mix * _rms_scale(mix) * post_ref[...]

    if not last:
        @pl.when(i == 0)
        def _():
            hm = mixed(hm_ref[...], oym_ref[...])
            hmout_ref[...] = hm
            hnm_ref[...] = (hm * _rms_scale(hm) * pre_ref[...]).astype(_BF16)

    if first:
        h_in = jnp.where(i < n_prompt_tiles, hp_ref[...], hs_ref[...])
    else:
        h_in = h_ref[...]
    h = mixed(h_in, oy_ref[...])
    if last:
        @pl.when(i < n_prompt_tiles)
        def _():
            yp_ref[...] = h

        @pl.when(i >= n_prompt_tiles)
        def _():
            ys_ref[...] = h
    else:
        hout_ref[...] = h
        hn_ref[...] = (h * _rms_scale(h) * pre_ref[...]).astype(_BF16)


def _out(h_args, hm, oy, oym, wo16, post_w, pre_w, *, layer, first, last):
    R, W2 = oy.shape
    D = wo16.shape[1]
    n_meta = oym.shape[0]
    n_p = R // ROW_TILE - 1
    kern = functools.partial(_out_kernel, n_prompt_tiles=n_p, first=first, last=last)
    tile_spec = pl.BlockSpec((ROW_TILE, D), lambda i: (i, 0))
    prompt_spec = pl.BlockSpec((ROW_TILE, D), lambda i: (jnp.minimum(i, n_p - 1), 0))
    sample_spec = pl.BlockSpec((ROW_TILE, D), lambda i: (0, 0))
    meta_spec = pl.BlockSpec((n_meta, D), lambda i: (0, 0))
    in_specs = ([prompt_spec, sample_spec] if first else [tile_spec]) + [
        meta_spec,
        pl.BlockSpec((ROW_TILE, W2), lambda i: (i, 0)),
        pl.BlockSpec((n_meta, W2), lambda i: (0, 0)),
        pl.BlockSpec((W2, D), lambda i: (0, 0), pipeline_mode=pl.Buffered(1)),
        pl.BlockSpec((None, 1, D), lambda i: (layer, 0, 0)),
    ]
    args = list(h_args) + [hm, oy, oym, wo16, post_w]
    if last:
        out_specs = [prompt_spec, sample_spec]
        out_shape = [jax.ShapeDtypeStruct((n_p * ROW_TILE, D), _F32),
                     jax.ShapeDtypeStruct((ROW_TILE, D), _F32)]
    else:
        in_specs.append(pl.BlockSpec((None, 1, D), lambda i: (layer + 1, 0, 0)))
        args.append(pre_w)
        out_specs = [tile_spec, tile_spec, meta_spec, meta_spec]
        out_shape = [jax.ShapeDtypeStruct((R, D), _F32), jax.ShapeDtypeStruct((R, D), _BF16),
                     jax.ShapeDtypeStruct((n_meta, D), _F32), jax.ShapeDtypeStruct((n_meta, D), _BF16)]
    return pl.pallas_call(
        kern, grid=(n_p + 1,), in_specs=in_specs, out_specs=out_specs, out_shape=out_shape,
        compiler_params=pltpu.CompilerParams(dimension_semantics=("arbitrary",),
                                             vmem_limit_bytes=VMEM_LIMIT),
        name="out_proj",
    )(*args)


def kernel(x_prompt, x_sample, state_hgrn, state_conv, meta_tokens, w_in, conv_w, lb_param,
           hgrn_norm_w, w_out, pre_norm_w, post_norm_w):
    B, T, D = x_prompt.shape
    DB, DT, _ = x_sample.shape
    depth = w_in.shape[0]
    n_meta = meta_tokens.shape[0]
    W = D // 2
    H = W // LANES
    assert W % LANES == 0 and SUBLANES % DT == 0 and n_meta % (2 * SUBLANES) == 0
    assert T % ROW_TILE == 0 and DB * DT == ROW_TILE and n_meta <= PROMPT_CHUNK
    assert w_in.shape == (depth, D, N_GROUPS * W) and state_conv.shape[2] == CONV_W - 1

    pre = pre_norm_w.reshape(depth, 1, D)
    post = post_norm_w.reshape(depth, 1, D)
    n_rows = depth + CONV_W + 1
    head_params = jnp.concatenate(
        [jnp.broadcast_to(lb_param[None], (depth, depth, W)), conv_w,
         jnp.tile(hgrn_norm_w, (1, H))[:, None, :],
         jnp.zeros((depth, -n_rows % SUBLANES, W), _F32)], axis=1)
    w_in_rows = w_in.reshape(depth, D * N_GROUPS, W)
    xp = x_prompt.reshape(B * T, D)
    xs = x_sample.reshape(DB * DT, D)
    cpad = jnp.concatenate(
        [state_conv, jnp.zeros((depth, DB, DT - (CONV_W - 1), W), _F32)], axis=2
    ).reshape(depth, DB * DT, W)
    tiles_per_seq = T // ROW_TILE
    last_tiles = slice(tiles_per_seq - 1, None, tiles_per_seq)

    hn, hnm = _first_norm(xp, xs, meta_tokens, pre)
    h_args, hm = (xp, xs), meta_tokens
    s_sample = None
    s_list, c_list, u_list = [], [], []
    for layer in range(depth):
        oy, oym, s_p, c_p, s_sample, u_s, wo16 = _mixer(
            hnm, hn, w_in_rows, w_out, head_params, state_hgrn, cpad, s_sample,
            layer=layer, tiles_per_seq=tiles_per_seq, seq_len=DT)
        s_list.append(s_p[last_tiles])
        c_list.append(c_p[last_tiles, SUBLANES - (CONV_W - 1):])
        u_list.append(u_s.reshape(DB, DT, W)[:, DT - (CONV_W - 1):])
        last = layer + 1 == depth
        outs = _out(h_args, hm, oy, oym, wo16, post, pre, layer=layer, first=layer == 0, last=last)
        if not last:
            h_all, hn, hm, hnm = outs
            h_args = (h_all,)
    y_prompt = outs[0].reshape(B, T, D)
    y_sample = outs[1].reshape(DB, DT, D)
    return (y_prompt, y_sample, jnp.stack(s_list), jnp.stack(c_list), s_sample, jnp.stack(u_list))
```

```python
import functools

import jax
import jax.numpy as jnp
from jax import lax
from jax.experimental import pallas as pl
from jax.experimental.pallas import tpu as pltpu

LANES = 128
SUBLANES = 8
N_GROUPS = 8
CONV_W = 3
EPS = 1e-6
F_FLOOR = 1e-30
LOG2_E = 1.4426950408889634
ROW_TILE = 512
PROMPT_CHUNK = 128
VMEM_LIMIT = 56 * 1024 * 1024

_F32 = jnp.float32
_BF16 = jnp.bfloat16
_NT = (((1,), (1,)), ((), ()))
_O_COLS = slice(0, LANES)
_YB_COLS = slice(LANES, 2 * LANES)


def _sigmoid(x):
    return 1.0 / (1.0 + jnp.exp(-x))


def _silu(x):
    return x * _sigmoid(x)


def _rms_scale(x):
    return lax.rsqrt(jnp.mean(x * x, axis=-1, keepdims=True) + EPS)


def _lower_bound(lbp, layer):
    e = jnp.exp(lbp - jnp.max(lbp, axis=0, keepdims=True))
    w = e / jnp.sum(e, axis=0, keepdims=True)
    r = lax.broadcasted_iota(jnp.int32, w.shape, 0)
    return jnp.sum(jnp.where((r >= 1) & (r <= layer), w, 0.0), axis=0, keepdims=True)


def _forget_gates(fz, lb):
    one_m = 1.0 - lb
    f = lb + one_m * _sigmoid(fz)
    return jnp.log(jnp.maximum(f, F_FLOOR)), one_m * _sigmoid(-fz)


def _shift_rows(x, s):
    return pltpu.roll(x, s % x.shape[0], 0)


def _cumsum_rows(x, row):
    s = 1
    while s < x.shape[0]:
        x = x + jnp.where(row >= s, _shift_rows(x, s), 0.0)
        s *= 2
    return x


def _anchor_rows(g_ref, G, n, row):
    C = G.shape[0]
    half = n // 2
    if n == 2:
        return jnp.where((row & 1) == 1, _shift_rows(G, 1), G)
    if n == 4:
        pos = row & 3
        return jnp.where(pos == 0, _shift_rows(G, -1),
                         jnp.where(pos == 1, G,
                                   jnp.where(pos == 2, _shift_rows(G, 1), _shift_rows(G, 2))))
    parts = []
    for a in range(C // n):
        r = g_ref[pl.ds(a * n + half - 1, 1), :]
        parts.append(jnp.broadcast_to(r, (n, LANES)))
    return parts[0] if len(parts) == 1 else jnp.concatenate(parts, axis=0)


def _intra_chunk_scores(q, kk, G2, g_ref, row, col):
    C = q.shape[0]
    A = jnp.where(row == col, jnp.sum(q * kk, axis=-1, keepdims=True), 0.0)
    q16 = q.astype(_BF16)
    k16 = kk.astype(_BF16)
    n = 2
    while n <= C:
        half = n // 2
        E = jnp.exp2(-jnp.abs(G2 - _anchor_rows(g_ref, G2, n, row))).astype(_BF16)
        P = lax.dot_general(q16 * E, k16 * E, _NT, preferred_element_type=_F32)
        shift = n.bit_length() - 1
        mask = ((row >> shift) == (col >> shift)) & ((row & (n - 1)) >= half) & ((col & (n - 1)) < half)
        A = jnp.where(mask, P, A)
        n *= 2
    return A


def _group(ref, g, rows=slice(None)):
    return ref[rows, g * LANES:(g + 1) * LANES]


def _conv_gate(bg, zb, u, um1, um2, cw):
    c = cw[0:1] * um2 + cw[1:2] * um1 + cw[2:3] * u
    return bg * c * _silu(zb)


def _chunked_head_mixer(proj_ref, lb, nw, cw, st_ref, cb_ref, g_ref, chunk, oy_ref, filler=None):
    TM = proj_ref.shape[0]
    C = chunk
    n_chunks = TM // C
    row = lax.broadcasted_iota(jnp.int32, (C, 1), 0)
    col = lax.broadcasted_iota(jnp.int32, (1, C), 1)

    def scores(c):
        rows = slice(c * C, (c + 1) * C)
        q = _group(proj_ref, 0, rows)
        v16 = _group(proj_ref, 2, rows).astype(_BF16)
        g, kk = _forget_gates(_group(proj_ref, 1, rows), lb)
        G2 = _cumsum_rows(g * LOG2_E, row)
        g_ref[0:C] = G2
        A = _intra_chunk_scores(q, kk, G2, g_ref, row, col).astype(_BF16)
        GL2 = G2[C - 1:C]
        return (A, v16, (q * jnp.exp2(G2)).astype(_BF16), (kk * jnp.exp2(GL2 - G2)).astype(_BF16),
                jnp.exp2(GL2))

    def advance(c, staged):
        A, v16, qdec, khat, decay = staged
        rows = slice(c * C, (c + 1) * C)
        St = st_ref[...]
        o = jnp.dot(A, v16, preferred_element_type=_F32)
        o = o + lax.dot_general(qdec, St.astype(_BF16), _NT, preferred_element_type=_F32)
        st_ref[...] = St * decay + pl.dot(v16, khat, trans_a=True)
        o = o * _rms_scale(o) * nw * _silu(_group(proj_ref, 3, rows))
        oy_ref[rows, _O_COLS] = o.astype(oy_ref.dtype)

    for c in range(n_chunks):
        if filler is not None:
            filler(c, n_chunks)
        advance(c, scores(c))

    u = _group(proj_ref, 5) * _group(proj_ref, 6)
    prev = cb_ref[...]
    trow = lax.broadcasted_iota(jnp.int32, (TM, 1), 0)
    um1 = jnp.where(trow >= 1, _shift_rows(u, 1), prev[7:8])
    um2 = jnp.where(trow >= 2, _shift_rows(u, 2), jnp.where(trow == 0, prev[6:7], prev[7:8]))
    oy_ref[:, _YB_COLS] = _conv_gate(_group(proj_ref, 4), _group(proj_ref, 7), u, um1, um2,
                                     cw).astype(oy_ref.dtype)
    cb_ref[...] = u[TM - SUBLANES:TM]


def _sample_prepare(proj_ref, lb, cw, cpad_ref, u_ref, qt_ref, kh_ref, dec_ref, v_ref, os_ref, ybs_ref, seq_len):
    T = seq_len
    TM = proj_ref.shape[0]
    row = lax.broadcasted_iota(jnp.int32, (TM, 1), 0)
    pos = row % T
    q = _group(proj_ref, 0)
    v = _group(proj_ref, 2)
    g, kk = _forget_gates(_group(proj_ref, 1), lb)

    G = g
    for d in range(1, T):
        G = G + jnp.where(pos >= d, _shift_rows(g, d), 0.0)
    o = jnp.sum(q * kk, axis=-1, keepdims=True) * v
    for d in range(1, T):
        decay = jnp.exp(jnp.minimum(G - _shift_rows(G, d), 0.0))
        a = jnp.sum(jnp.where(pos >= d, q * _shift_rows(kk, d) * decay, 0.0), axis=-1, keepdims=True)
        o = o + a * _shift_rows(v, d)
    GL = G
    for d in range(1, T):
        GL = jnp.where(pos == T - 1 - d, _shift_rows(G, -d), GL)
    qt_ref[...] = q * jnp.exp(G)
    kh_ref[...] = kk * jnp.exp(GL - G)
    dec_ref[...] = jnp.exp(GL)
    v_ref[...] = v
    os_ref[...] = o

    u = _group(proj_ref, 5) * _group(proj_ref, 6)
    e = cpad_ref[...]
    um1 = jnp.where(pos >= 1, _shift_rows(u, 1), _shift_rows(e, -1))
    um2 = jnp.where(pos >= 2, _shift_rows(u, 2), e)
    ybs_ref[...] = _conv_gate(_group(proj_ref, 4), _group(proj_ref, 7), u, um1, um2, cw).astype(ybs_ref.dtype)
    u_ref[...] = u


def _sample_group(group, local_group, sin_ref, sout_ref, qt_ref, kh_ref, dec_ref, v_ref, oi_ref, seq_len):
    T = seq_len
    per_group = SUBLANES // T
    row8 = lax.broadcasted_iota(jnp.int32, (SUBLANES, 1), 0)
    r = pl.multiple_of(group * SUBLANES, SUBLANES)
    q8 = qt_ref[pl.ds(r, SUBLANES), :].astype(_BF16)
    k8 = kh_ref[pl.ds(r, SUBLANES), :]
    v8 = v_ref[pl.ds(r, SUBLANES), :].astype(_BF16)
    d8 = dec_ref[pl.ds(r, SUBLANES), :]
    o8 = jnp.zeros((SUBLANES, LANES), _F32)
    for i in range(per_group):
        b = local_group * per_group + i
        mine = (row8 >= i * T) & (row8 < (i + 1) * T)
        S0 = sin_ref[b]
        o8 = jnp.where(mine, jnp.dot(q8, S0.astype(_BF16), preferred_element_type=_F32), o8)
        upd = pl.dot(jnp.where(mine, k8, 0.0).astype(_BF16), v8, trans_a=True)
        decay = jnp.broadcast_to(d8[i * T:i * T + 1], (LANES, LANES)).T
        sout_ref[b] = decay * S0 + upd
    oi_ref[pl.ds(r, SUBLANES), :] = o8


def _sample_finish(proj_ref, nw, os_ref, oi_ref, ybs_ref, oy_ref):
    o = os_ref[...] + oi_ref[...]
    o = o * _rms_scale(o) * nw * _silu(_group(proj_ref, 3))
    oy_ref[:, _O_COLS] = o.astype(oy_ref.dtype)
    oy_ref[:, _YB_COLS] = ybs_ref[...]


def _first_norm_kernel(xp_ref, xs_ref, xm_ref, w_ref, hn_ref, hnm_ref, *, n_prompt_tiles):
    i = pl.program_id(0)
    x = jnp.where(i < n_prompt_tiles, xp_ref[...], xs_ref[...])
    hn_ref[...] = (x * _rms_scale(x) * w_ref[...]).astype(_BF16)

    @pl.when(i == 0)
    def _():
        xm = xm_ref[...]
        hnm_ref[...] = (xm * _rms_scale(xm) * w_ref[...]).astype(_BF16)


def _first_norm(xp, xs, xm, pre_w):
    Rp, D = xp.shape
    n_p = Rp // ROW_TILE
    n_meta = xm.shape[0]
    return pl.pallas_call(
        functools.partial(_first_norm_kernel, n_prompt_tiles=n_p), grid=(n_p + 1,),
        in_specs=[pl.BlockSpec((ROW_TILE, D), lambda i: (jnp.minimum(i, n_p - 1), 0)),
                  pl.BlockSpec((ROW_TILE, D), lambda i: (0, 0)),
                  pl.BlockSpec((n_meta, D), lambda i: (0, 0)),
                  pl.BlockSpec((None, 1, D), lambda i: (0, 0, 0))],
        out_specs=[pl.BlockSpec((ROW_TILE, D), lambda i: (i, 0)),
                   pl.BlockSpec((n_meta, D), lambda i: (0, 0))],
        out_shape=[jax.ShapeDtypeStruct((Rp + ROW_TILE, D), _BF16),
                   jax.ShapeDtypeStruct((n_meta, D), _BF16)],
        compiler_params=pltpu.CompilerParams(dimension_semantics=("arbitrary",),
                                             vmem_limit_bytes=VMEM_LIMIT),
        name="first_norm",
    )(xp, xs, xm, pre_w)


def _mixer_kernel(hnm_ref, hn_ref, hns_ref, *refs, layer, depth, n_prompt_tiles, tiles_per_seq, seq_len,
                  aliased):
    slabs = refs[:N_GROUPS]
    woa_ref, wob_ref = refs[N_GROUPS:N_GROUPS + 2]
    refs = refs[N_GROUPS + 2:]
    if aliased:
        refs = refs[1:]
    (prm_ref, sin_ref, cpad_ref,
     oy_ref, oym_ref, sp_ref, cp_ref, ss_ref, us_ref, wo16_ref,
     w_scr, proj_ref, next_ref, projm_ref, projs_ref, st_ref, cb_ref, st0_ref, cb0_ref, g_ref,
     qt_ref, kh_ref, dec_ref, v_ref, os_ref, oi_ref, ybs_ref) = refs
    s = pl.program_id(1)
    n_meta = hnm_ref.shape[0]
    groups_per_step = sin_ref.shape[0] * seq_len // SUBLANES
    lb = _lower_bound(prm_ref[0:depth], layer)
    cw = prm_ref[depth:depth + CONV_W]
    nw = prm_ref[depth + CONV_W:depth + CONV_W + 1]

    @pl.when(s == 0)
    def _():
        for g, slab in enumerate(slabs):
            w_scr[:, g * LANES:(g + 1) * LANES] = slab[...].astype(_BF16)
        wo16_ref[_O_COLS.start:_O_COLS.stop] = woa_ref[...].astype(_BF16)
        wo16_ref[_YB_COLS.start:_YB_COLS.stop] = wob_ref[...].astype(_BF16)
        projm_ref[...] = jnp.dot(hnm_ref[...], w_scr[...], preferred_element_type=_F32)
        st_ref[...] = jnp.zeros_like(st_ref)
        cb_ref[...] = jnp.zeros_like(cb_ref)
        _chunked_head_mixer(projm_ref, lb, nw, cw, st_ref, cb_ref, g_ref, n_meta, oym_ref)
        st0_ref[...] = st_ref[...]
        cb0_ref[...] = cb_ref[...]
        projs_ref[...] = jnp.dot(hns_ref[...], w_scr[...], preferred_element_type=_F32)
        _sample_prepare(projs_ref, lb, cw, cpad_ref, us_ref, qt_ref, kh_ref, dec_ref, v_ref, os_ref, ybs_ref,
                        seq_len)
        proj_ref[...] = jnp.dot(hn_ref[...], w_scr[...], preferred_element_type=_F32)

    def prompt_step(project_next):
        @pl.when((s - 1) % tiles_per_seq == 0)
        def _():
            st_ref[...] = st0_ref[...]
            cb_ref[...] = cb0_ref[...]

        def filler(c, n):
            if project_next:
                width = N_GROUPS * LANES // n
                cols = slice(c * width, (c + 1) * width)
                next_ref[:, cols] = jnp.dot(hn_ref[...], w_scr[:, cols], preferred_element_type=_F32)
            for i in range(c * groups_per_step // n, (c + 1) * groups_per_step // n):
                _sample_group((s - 1) * groups_per_step + i, i, sin_ref, ss_ref, qt_ref, kh_ref, dec_ref,
                              v_ref, oi_ref, seq_len)

        _chunked_head_mixer(proj_ref, lb, nw, cw, st_ref, cb_ref, g_ref, PROMPT_CHUNK, oy_ref, filler=filler)
        cp_ref[...] = cb_ref[...]
        sp_ref[...] = st_ref[...].T
        if project_next:
            proj_ref[...] = next_ref[...]

    @pl.when((s >= 1) & (s < n_prompt_tiles))
    def _():
        prompt_step(True)

    @pl.when(s == n_prompt_tiles)
    def _():
        prompt_step(False)

    @pl.when(s > n_prompt_tiles)
    def _():
        _sample_finish(projs_ref, nw, os_ref, oi_ref, ybs_ref, oy_ref)


def _mixer(hnm, hn, w_in, w_out, head_params, state, cpad, prev_states, *, layer, tiles_per_seq, seq_len):
    R, D = hn.shape
    n_meta = hnm.shape[0]
    depth = w_in.shape[0]
    W = w_in.shape[2] // N_GROUPS
    H = W // LANES
    n_p = R // ROW_TILE - 1
    n_seq = state.shape[1]
    seq_per_step = n_seq // n_p
    assert seq_per_step * n_p == n_seq and (seq_per_step * seq_len) % (SUBLANES * (ROW_TILE // PROMPT_CHUNK)) == 0
    n_steps = n_p + 2
    aliased = prev_states is not None

    def next_tile_of(s):
        return jnp.clip(s, 0, n_p - 1)

    def out_tile_of(s):
        return jnp.clip(s - 1, 0, n_p)

    def prompt_tile_of(s):
        return jnp.clip(s - 1, 0, n_p - 1)

    kern = functools.partial(_mixer_kernel, layer=layer, depth=depth, n_prompt_tiles=n_p,
                             tiles_per_seq=tiles_per_seq, seq_len=seq_len, aliased=aliased)
    seq_state_spec = pl.BlockSpec((None, seq_per_step, None, LANES, LANES),
                                  lambda j, s: (layer, prompt_tile_of(s), j, 0, 0))
    in_specs = (
        [pl.BlockSpec((n_meta, D), lambda j, s: (0, 0)),
         pl.BlockSpec((ROW_TILE, D), lambda j, s: (next_tile_of(s), 0)),
         pl.BlockSpec((ROW_TILE, D), lambda j, s: (n_p, 0))]
        + [pl.BlockSpec((None, D, LANES), lambda j, s, g=g: (layer, 0, g * H + j)) for g in range(N_GROUPS)]
        + [pl.BlockSpec((None, LANES, D), lambda j, s: (layer, j, 0)),
           pl.BlockSpec((None, LANES, D), lambda j, s: (layer, H + j, 0))]
        + ([pl.BlockSpec(memory_space=pl.ANY)] if aliased else [])
        + [pl.BlockSpec((None, head_params.shape[1], LANES), lambda j, s: (layer, 0, j)),
           seq_state_spec,
           pl.BlockSpec((None, ROW_TILE, LANES), lambda j, s: (layer, 0, j))])
    out_specs = [
        pl.BlockSpec((ROW_TILE, 2 * LANES), lambda j, s: (out_tile_of(s), j)),
        pl.BlockSpec((n_meta, 2 * LANES), lambda j, s: (0, j)),
        pl.BlockSpec((None, None, LANES, LANES), lambda j, s: (prompt_tile_of(s), j, 0, 0)),
        pl.BlockSpec((None, SUBLANES, LANES), lambda j, s: (prompt_tile_of(s), 0, j)),
        seq_state_spec,
        pl.BlockSpec((ROW_TILE, LANES), lambda j, s: (0, j)),
        pl.BlockSpec((2 * LANES, D), lambda j, s: (j, 0)),
    ]
    out_shape = [
        jax.ShapeDtypeStruct((R, 2 * W), _BF16),
        jax.ShapeDtypeStruct((n_meta, 2 * W), _BF16),
        jax.ShapeDtypeStruct((n_p, H, LANES, LANES), _F32),
        jax.ShapeDtypeStruct((n_p, SUBLANES, W), _F32),
        jax.ShapeDtypeStruct(state.shape, _F32),
        jax.ShapeDtypeStruct((ROW_TILE, W), _F32),
        jax.ShapeDtypeStruct((2 * W, D), _BF16),
    ]
    scratch = [
        pltpu.VMEM((D, N_GROUPS * LANES), _BF16),
        pltpu.VMEM((ROW_TILE, N_GROUPS * LANES), _F32),
        pltpu.VMEM((ROW_TILE, N_GROUPS * LANES), _F32),
        pltpu.VMEM((n_meta, N_GROUPS * LANES), _F32),
        pltpu.VMEM((ROW_TILE, N_GROUPS * LANES), _F32),
        pltpu.VMEM((LANES, LANES), _F32),
        pltpu.VMEM((SUBLANES, LANES), _F32),
        pltpu.VMEM((LANES, LANES), _F32),
        pltpu.VMEM((SUBLANES, LANES), _F32),
        pltpu.VMEM((PROMPT_CHUNK, LANES), _F32),
    ] + [pltpu.VMEM((ROW_TILE, LANES), _F32)] * 6 + [
        pltpu.VMEM((ROW_TILE, LANES), _BF16)]
    args = ([hnm, hn, hn] + [w_in] * N_GROUPS + [w_out, w_out] + ([prev_states] if aliased else [])
            + [head_params, state, cpad])
    return pl.pallas_call(
        kern, grid=(H, n_steps), in_specs=in_specs, out_specs=out_specs, out_shape=out_shape,
        scratch_shapes=scratch,
        input_output_aliases={N_GROUPS + 5: 4} if aliased else {},
        compiler_params=pltpu.CompilerParams(dimension_semantics=("arbitrary", "arbitrary"),
                                             vmem_limit_bytes=VMEM_LIMIT),
        name="mixer",
    )(*args)


def _out_kernel(*refs, n_prompt_tiles, first, last):
    refs = list(refs)
    if first:
        hp_ref, hs_ref = refs[:2]
        refs = refs[2:]
    else:
        h_ref = refs.pop(0)
    hm_ref, oy_ref, oym_ref, wo_ref, post_ref = refs[:5]
    refs = refs[5:]
    if last:
        yp_ref, ys_ref = refs
    else:
        pre_ref, hout_ref, hn_ref, hmout_ref, hnm_ref = refs
    i = pl.program_id(0)

    def mixed(h, oy):
        mix = jnp.dot(oy, wo_ref[...], preferred_element_type=_F32)
        return h + mix * _rms_scale(mix) * post_ref[...]

    if not last:
        @pl.when(i == 0)
        def _():
            hm = mixed(hm_ref[...], oym_ref[...])
            hmout_ref[...] = hm
            hnm_ref[...] = (hm * _rms_scale(hm) * pre_ref[...]).astype(_BF16)

    if first:
        h_in = jnp.where(i < n_prompt_tiles, hp_ref[...], hs_ref[...])
    else:
        h_in = h_ref[...]
    h = mixed(h_in, oy_ref[...])
    if last:
        @pl.when(i < n_prompt_tiles)
        def _():
            yp_ref[...] = h

        @pl.when(i >= n_prompt_tiles)
        def _():
            ys_ref[...] = h
    else:
        hout_ref[...] = h
        hn_ref[...] = (h * _rms_scale(h) * pre_ref[...]).astype(_BF16)


def _out(h_args, hm, oy, oym, wo16, post_w, pre_w, *, layer, first, last):
    R, W2 = oy.shape
    D = wo16.shape[1]
    n_meta = oym.shape[0]
    n_p = R // ROW_TILE - 1
    kern = functools.partial(_out_kernel, n_prompt_tiles=n_p, first=first, last=last)
    tile_spec = pl.BlockSpec((ROW_TILE, D), lambda i: (i, 0))
    prompt_spec = pl.BlockSpec((ROW_TILE, D), lambda i: (jnp.minimum(i, n_p - 1), 0))
    sample_spec = pl.BlockSpec((ROW_TILE, D), lambda i: (0, 0))
    meta_spec = pl.BlockSpec((n_meta, D), lambda i: (0, 0))
    in_specs = ([prompt_spec, sample_spec] if first else [tile_spec]) + [
        meta_spec,
        pl.BlockSpec((ROW_TILE, W2), lambda i: (i, 0)),
        pl.BlockSpec((n_meta, W2), lambda i: (0, 0)),
        pl.BlockSpec((W2, D), lambda i: (0, 0), pipeline_mode=pl.Buffered(1)),
        pl.BlockSpec((None, 1, D), lambda i: (layer, 0, 0)),
    ]
    args = list(h_args) + [hm, oy, oym, wo16, post_w]
    if last:
        out_specs = [prompt_spec, sample_spec]
        out_shape = [jax.ShapeDtypeStruct((n_p * ROW_TILE, D), _F32),
                     jax.ShapeDtypeStruct((ROW_TILE, D), _F32)]
    else:
        in_specs.append(pl.BlockSpec((None, 1, D), lambda i: (layer + 1, 0, 0)))
        args.append(pre_w)
        out_specs = [tile_spec, tile_spec, meta_spec, meta_spec]
        out_shape = [jax.ShapeDtypeStruct((R, D), _F32), jax.ShapeDtypeStruct((R, D), _BF16),
                     jax.ShapeDtypeStruct((n_meta, D), _F32), jax.ShapeDtypeStruct((n_meta, D), _BF16)]
    return pl.pallas_call(
        kern, grid=(n_p + 1,), in_specs=in_specs, out_specs=out_specs, out_shape=out_shape,
        compiler_params=pltpu.CompilerParams(dimension_semantics=("arbitrary",),
                                             vmem_limit_bytes=VMEM_LIMIT),
        name="out_proj",
    )(*args)


def kernel(x_prompt, x_sample, state_hgrn, state_conv, meta_tokens, w_in, conv_w, lb_param,
           hgrn_norm_w, w_out, pre_norm_w, post_norm_w):
    B, T, D = x_prompt.shape
    DB, DT, _ = x_sample.shape
    depth = w_in.shape[0]
    n_meta = meta_tokens.shape[0]
    W = D // 2
    H = W // LANES
    assert W % LANES == 0 and SUBLANES % DT == 0 and n_meta % (2 * SUBLANES) == 0
    assert T % ROW_TILE == 0 and DB * DT == ROW_TILE and n_meta <= PROMPT_CHUNK
    assert w_in.shape == (depth, D, N_GROUPS * W) and state_conv.shape[2] == CONV_W - 1

    pre = pre_norm_w.reshape(depth, 1, D)
    post = post_norm_w.reshape(depth, 1, D)
    n_rows = depth + CONV_W + 1
    head_params = jnp.concatenate(
        [jnp.broadcast_to(lb_param[None], (depth, depth, W)), conv_w,
         jnp.tile(hgrn_norm_w, (1, H))[:, None, :],
         jnp.zeros((depth, -n_rows % SUBLANES, W), _F32)], axis=1)
    xp = x_prompt.reshape(B * T, D)
    xs = x_sample.reshape(DB * DT, D)
    cpad = jnp.concatenate(
        [state_conv, jnp.zeros((depth, DB, DT - (CONV_W - 1), W), _F32)], axis=2
    ).reshape(depth, DB * DT, W)
    tiles_per_seq = T // ROW_TILE
    last_tiles = slice(tiles_per_seq - 1, None, tiles_per_seq)

    hn, hnm = _first_norm(xp, xs, meta_tokens, pre)
    h_args, hm = (xp, xs), meta_tokens
    s_sample = None
    s_list, c_list, u_list = [], [], []
    for layer in range(depth):
        oy, oym, s_p, c_p, s_sample, u_s, wo16 = _mixer(
            hnm, hn, w_in, w_out, head_params, state_hgrn, cpad, s_sample,
            layer=layer, tiles_per_seq=tiles_per_seq, seq_len=DT)
        s_list.append(s_p[last_tiles])
        c_list.append(c_p[last_tiles, SUBLANES - (CONV_W - 1):])
        u_list.append(u_s.reshape(DB, DT, W)[:, DT - (CONV_W - 1):])
        last = layer + 1 == depth
        outs = _out(h_args, hm, oy, oym, wo16, post, pre, layer=layer, first=layer == 0, last=last)
        if not last:
            h_all, hn, hm, hnm = outs
            h_args = (h_all,)
    y_prompt = outs[0].reshape(B, T, D)
    y_sample = outs[1].reshape(DB, DT, D)
    return (y_prompt, y_sample, jnp.stack(s_list), jnp.stack(c_list), s_sample, jnp.stack(u_list))
```

```python
import functools

import jax
import jax.numpy as jnp
from jax import lax
from jax.experimental import pallas as pl
from jax.experimental.pallas import tpu as pltpu

LANES = 128
SUBLANES = 8
N_GROUPS = 8
CONV_W = 3
EPS = 1e-6
F_FLOOR = 1e-30
LOG2_E = 1.4426950408889634
ROW_TILE = 512
PROMPT_CHUNK = 128
VMEM_LIMIT = 56 * 1024 * 1024

_F32 = jnp.float32
_BF16 = jnp.bfloat16
_NT = (((1,), (1,)), ((), ()))
_O_COLS = slice(0, LANES)
_YB_COLS = slice(LANES, 2 * LANES)


def _sigmoid(x):
    return 1.0 / (1.0 + jnp.exp(-x))


def _silu(x):
    return x * _sigmoid(x)


def _rms_scale(x):
    return lax.rsqrt(jnp.mean(x * x, axis=-1, keepdims=True) + EPS)


def _lower_bound(lbp, layer):
    e = jnp.exp(lbp - jnp.max(lbp, axis=0, keepdims=True))
    w = e / jnp.sum(e, axis=0, keepdims=True)
    r = lax.broadcasted_iota(jnp.int32, w.shape, 0)
    return jnp.sum(jnp.where((r >= 1) & (r <= layer), w, 0.0), axis=0, keepdims=True)


def _forget_gates(fz, lb):
    one_m = 1.0 - lb
    f = lb + one_m * _sigmoid(fz)
    return jnp.log(jnp.maximum(f, F_FLOOR)), one_m * _sigmoid(-fz)


def _shift_rows(x, s):
    return pltpu.roll(x, s % x.shape[0], 0)


def _cumsum_rows(x, row):
    s = 1
    while s < x.shape[0]:
        x = x + jnp.where(row >= s, _shift_rows(x, s), 0.0)
        s *= 2
    return x


def _anchor_rows(g_ref, G, n, row):
    C = G.shape[0]
    half = n // 2
    if n == 2:
        return jnp.where((row & 1) == 1, _shift_rows(G, 1), G)
    if n == 4:
        pos = row & 3
        return jnp.where(pos == 0, _shift_rows(G, -1),
                         jnp.where(pos == 1, G,
                                   jnp.where(pos == 2, _shift_rows(G, 1), _shift_rows(G, 2))))
    parts = []
    for a in range(C // n):
        r = g_ref[pl.ds(a * n + half - 1, 1), :]
        parts.append(jnp.broadcast_to(r, (n, LANES)))
    return parts[0] if len(parts) == 1 else jnp.concatenate(parts, axis=0)


def _intra_chunk_scores(q, kk, G2, g_ref, row, col):
    C = q.shape[0]
    A = jnp.where(row == col, jnp.sum(q * kk, axis=-1, keepdims=True), 0.0)
    q16 = q.astype(_BF16)
    k16 = kk.astype(_BF16)
    n = 2
    while n <= C:
        half = n // 2
        E = jnp.exp2(-jnp.abs(G2 - _anchor_rows(g_ref, G2, n, row))).astype(_BF16)
        P = lax.dot_general(q16 * E, k16 * E, _NT, preferred_element_type=_F32)
        shift = n.bit_length() - 1
        mask = ((row >> shift) == (col >> shift)) & ((row & (n - 1)) >= half) & ((col & (n - 1)) < half)
        A = jnp.where(mask, P, A)
        n *= 2
    return A


def _group(ref, g, rows=slice(None)):
    return ref[rows, g * LANES:(g + 1) * LANES]


def _conv_gate(bg, zb, u, um1, um2, cw):
    c = cw[0:1] * um2 + cw[1:2] * um1 + cw[2:3] * u
    return bg * c * _silu(zb)


def _chunked_head_mixer(proj_ref, lb, nw, cw, st_ref, cb_ref, g_ref, chunk, oy_ref, filler=None):
    TM = proj_ref.shape[0]
    C = chunk
    n_chunks = TM // C
    row = lax.broadcasted_iota(jnp.int32, (C, 1), 0)
    col = lax.broadcasted_iota(jnp.int32, (1, C), 1)

    def scores(c):
        rows = slice(c * C, (c + 1) * C)
        q = _group(proj_ref, 0, rows)
        v16 = _group(proj_ref, 2, rows).astype(_BF16)
        g, kk = _forget_gates(_group(proj_ref, 1, rows), lb)
        G2 = _cumsum_rows(g * LOG2_E, row)
        g_ref[0:C] = G2
        A = _intra_chunk_scores(q, kk, G2, g_ref, row, col).astype(_BF16)
        GL2 = G2[C - 1:C]
        return (A, v16, (q * jnp.exp2(G2)).astype(_BF16), (kk * jnp.exp2(GL2 - G2)).astype(_BF16),
                jnp.exp2(GL2))

    def advance(c, staged):
        A, v16, qdec, khat, decay = staged
        rows = slice(c * C, (c + 1) * C)
        St = st_ref[...]
        o = jnp.dot(A, v16, preferred_element_type=_F32)
        o = o + lax.dot_general(qdec, St.astype(_BF16), _NT, preferred_element_type=_F32)
        st_ref[...] = St * decay + pl.dot(v16, khat, trans_a=True)
        o = o * _rms_scale(o) * nw * _silu(_group(proj_ref, 3, rows))
        oy_ref[rows, _O_COLS] = o.astype(oy_ref.dtype)

    for c in range(n_chunks):
        if filler is not None:
            filler(c, n_chunks)
        advance(c, scores(c))

    u = _group(proj_ref, 5) * _group(proj_ref, 6)
    prev = cb_ref[...]
    trow = lax.broadcasted_iota(jnp.int32, (TM, 1), 0)
    um1 = jnp.where(trow >= 1, _shift_rows(u, 1), prev[7:8])
    um2 = jnp.where(trow >= 2, _shift_rows(u, 2), jnp.where(trow == 0, prev[6:7], prev[7:8]))
    oy_ref[:, _YB_COLS] = _conv_gate(_group(proj_ref, 4), _group(proj_ref, 7), u, um1, um2,
                                     cw).astype(oy_ref.dtype)
    cb_ref[...] = u[TM - SUBLANES:TM]


def _sample_prepare(proj_ref, lb, cw, cpad_ref, u_ref, qt_ref, kh_ref, dec_ref, v_ref, os_ref, ybs_ref, seq_len):
    T = seq_len
    TM = proj_ref.shape[0]
    row = lax.broadcasted_iota(jnp.int32, (TM, 1), 0)
    pos = row % T
    q = _group(proj_ref, 0)
    v = _group(proj_ref, 2)
    g, kk = _forget_gates(_group(proj_ref, 1), lb)

    G = g
    for d in range(1, T):
        G = G + jnp.where(pos >= d, _shift_rows(g, d), 0.0)
    o = jnp.sum(q * kk, axis=-1, keepdims=True) * v
    for d in range(1, T):
        decay = jnp.exp(jnp.minimum(G - _shift_rows(G, d), 0.0))
        a = jnp.sum(jnp.where(pos >= d, q * _shift_rows(kk, d) * decay, 0.0), axis=-1, keepdims=True)
        o = o + a * _shift_rows(v, d)
    GL = G
    for d in range(1, T):
        GL = jnp.where(pos == T - 1 - d, _shift_rows(G, -d), GL)
    qt_ref[...] = q * jnp.exp(G)
    kh_ref[...] = kk * jnp.exp(GL - G)
    dec_ref[...] = jnp.exp(GL)
    v_ref[...] = v
    os_ref[...] = o

    u = _group(proj_ref, 5) * _group(proj_ref, 6)
    e = cpad_ref[...]
    um1 = jnp.where(pos >= 1, _shift_rows(u, 1), _shift_rows(e, -1))
    um2 = jnp.where(pos >= 2, _shift_rows(u, 2), e)
    ybs_ref[...] = _conv_gate(_group(proj_ref, 4), _group(proj_ref, 7), u, um1, um2, cw).astype(ybs_ref.dtype)
    u_ref[...] = u


def _sample_group(group, local_group, sin_ref, sout_ref, qt_ref, kh_ref, dec_ref, v_ref, oi_ref, seq_len):
    T = seq_len
    per_group = SUBLANES // T
    row8 = lax.broadcasted_iota(jnp.int32, (SUBLANES, 1), 0)
    r = pl.multiple_of(group * SUBLANES, SUBLANES)
    q8 = qt_ref[pl.ds(r, SUBLANES), :].astype(_BF16)
    k8 = kh_ref[pl.ds(r, SUBLANES), :]
    v8 = v_ref[pl.ds(r, SUBLANES), :].astype(_BF16)
    d8 = dec_ref[pl.ds(r, SUBLANES), :]
    o8 = jnp.zeros((SUBLANES, LANES), _F32)
    for i in range(per_group):
        b = local_group * per_group + i
        mine = (row8 >= i * T) & (row8 < (i + 1) * T)
        S0 = sin_ref[b]
        o8 = jnp.where(mine, jnp.dot(q8, S0.astype(_BF16), preferred_element_type=_F32), o8)
        upd = pl.dot(jnp.where(mine, k8, 0.0).astype(_BF16), v8, trans_a=True)
        decay = jnp.broadcast_to(d8[i * T:i * T + 1], (LANES, LANES)).T
        sout_ref[b] = decay * S0 + upd
    oi_ref[pl.ds(r, SUBLANES), :] = o8


def _sample_finish(proj_ref, nw, os_ref, oi_ref, ybs_ref, oy_ref):
    o = os_ref[...] + oi_ref[...]
    o = o * _rms_scale(o) * nw * _silu(_group(proj_ref, 3))
    oy_ref[:, _O_COLS] = o.astype(oy_ref.dtype)
    oy_ref[:, _YB_COLS] = ybs_ref[...]


def _first_norm_kernel(xp_ref, xs_ref, xm_ref, w_ref, hn_ref, hnm_ref, *, n_prompt_tiles):
    i = pl.program_id(0)
    x = jnp.where(i < n_prompt_tiles, xp_ref[...], xs_ref[...])
    hn_ref[...] = (x * _rms_scale(x) * w_ref[...]).astype(_BF16)

    @pl.when(i == 0)
    def _():
        xm = xm_ref[...]
        hnm_ref[...] = (xm * _rms_scale(xm) * w_ref[...]).astype(_BF16)


def _first_norm(xp, xs, xm, pre_w):
    Rp, D = xp.shape
    n_p = Rp // ROW_TILE
    n_meta = xm.shape[0]
    return pl.pallas_call(
        functools.partial(_first_norm_kernel, n_prompt_tiles=n_p), grid=(n_p + 1,),
        in_specs=[pl.BlockSpec((ROW_TILE, D), lambda i: (jnp.minimum(i, n_p - 1), 0)),
                  pl.BlockSpec((ROW_TILE, D), lambda i: (0, 0)),
                  pl.BlockSpec((n_meta, D), lambda i: (0, 0)),
                  pl.BlockSpec((None, 1, D), lambda i: (0, 0, 0))],
        out_specs=[pl.BlockSpec((ROW_TILE, D), lambda i: (i, 0)),
                   pl.BlockSpec((n_meta, D), lambda i: (0, 0))],
        out_shape=[jax.ShapeDtypeStruct((Rp + ROW_TILE, D), _BF16),
                   jax.ShapeDtypeStruct((n_meta, D), _BF16)],
        compiler_params=pltpu.CompilerParams(dimension_semantics=("arbitrary",),
                                             vmem_limit_bytes=VMEM_LIMIT),
        name="first_norm",
    )(xp, xs, xm, pre_w)


def _mixer_kernel(hnm_ref, hn_ref, hns_ref, *refs, layer, depth, n_prompt_tiles, tiles_per_seq, seq_len,
                  aliased):
    slabs = refs[:N_GROUPS]
    woa_ref, wob_ref = refs[N_GROUPS:N_GROUPS + 2]
    refs = refs[N_GROUPS + 2:]
    if aliased:
        refs = refs[1:]
    (prm_ref, sin_ref, cpad_ref,
     oy_ref, oym_ref, sp_ref, cp_ref, ss_ref, us_ref, wo16_ref,
     w_scr, proj_ref, next_ref, projm_ref, projs_ref, st_ref, cb_ref, st0_ref, cb0_ref, g_ref,
     qt_ref, kh_ref, dec_ref, v_ref, os_ref, oi_ref, ybs_ref) = refs
    s = pl.program_id(1)
    n_meta = hnm_ref.shape[0]
    groups_per_step = sin_ref.shape[0] * seq_len // SUBLANES
    lb = _lower_bound(prm_ref[0:depth], layer)
    cw = prm_ref[depth:depth + CONV_W]
    nw = prm_ref[depth + CONV_W:depth + CONV_W + 1]

    @pl.when(s == 0)
    def _():
        for g, slab in enumerate(slabs):
            w_scr[:, g * LANES:(g + 1) * LANES] = slab[...].astype(_BF16)
        wo16_ref[_O_COLS.start:_O_COLS.stop] = woa_ref[...].astype(_BF16)
        wo16_ref[_YB_COLS.start:_YB_COLS.stop] = wob_ref[...].astype(_BF16)
        projm_ref[...] = jnp.dot(hnm_ref[...], w_scr[...], preferred_element_type=_F32)
        st_ref[...] = jnp.zeros_like(st_ref)
        cb_ref[...] = jnp.zeros_like(cb_ref)
        _chunked_head_mixer(projm_ref, lb, nw, cw, st_ref, cb_ref, g_ref, n_meta, oym_ref)
        st0_ref[...] = st_ref[...]
        cb0_ref[...] = cb_ref[...]
        projs_ref[...] = jnp.dot(hns_ref[...], w_scr[...], preferred_element_type=_F32)
        _sample_prepare(projs_ref, lb, cw, cpad_ref, us_ref, qt_ref, kh_ref, dec_ref, v_ref, os_ref, ybs_ref,
                        seq_len)
        proj_ref[...] = jnp.dot(hn_ref[...], w_scr[...], preferred_element_type=_F32)

    def prompt_step(project_next):
        @pl.when((s - 1) % tiles_per_seq == 0)
        def _():
            st_ref[...] = st0_ref[...]
            cb_ref[...] = cb0_ref[...]

        def filler(c, n):
            if project_next:
                width = N_GROUPS * LANES // n
                cols = slice(c * width, (c + 1) * width)
                next_ref[:, cols] = jnp.dot(hn_ref[...], w_scr[:, cols], preferred_element_type=_F32)
            for i in range(c * groups_per_step // n, (c + 1) * groups_per_step // n):
                _sample_group((s - 1) * groups_per_step + i, i, sin_ref, ss_ref, qt_ref, kh_ref, dec_ref,
                              v_ref, oi_ref, seq_len)

        _chunked_head_mixer(proj_ref, lb, nw, cw, st_ref, cb_ref, g_ref, PROMPT_CHUNK, oy_ref, filler=filler)
        cp_ref[...] = cb_ref[...]
        sp_ref[...] = st_ref[...].T
        if project_next:
            proj_ref[...] = next_ref[...]

    @pl.when((s >= 1) & (s < n_prompt_tiles))
    def _():
        prompt_step(True)

    @pl.when(s == n_prompt_tiles)
    def _():
        prompt_step(False)

    @pl.when(s > n_prompt_tiles)
    def _():
        _sample_finish(projs_ref, nw, os_ref, oi_ref, ybs_ref, oy_ref)


def _mixer(hnm, hn, w_in, w_out, head_params, state, cpad, prev_states, *, layer, tiles_per_seq, seq_len):
    R, D = hn.shape
    n_meta = hnm.shape[0]
    depth = w_in.shape[0]
    W = w_in.shape[2] // N_GROUPS
    H = W // LANES
    n_p = R // ROW_TILE - 1
    n_seq = state.shape[1]
    seq_per_step = n_seq // n_p
    assert seq_per_step * n_p == n_seq and (seq_per_step * seq_len) % (SUBLANES * (ROW_TILE // PROMPT_CHUNK)) == 0
    n_steps = n_p + 2
    aliased = prev_states is not None

    def weights_head(j, s):
        return jnp.minimum(j + (s >= 2).astype(jnp.int32), H - 1)

    def next_tile_of(s):
        return jnp.clip(s, 0, n_p - 1)

    def out_tile_of(s):
        return jnp.clip(s - 1, 0, n_p)

    def prompt_tile_of(s):
        return jnp.clip(s - 1, 0, n_p - 1)

    kern = functools.partial(_mixer_kernel, layer=layer, depth=depth, n_prompt_tiles=n_p,
                             tiles_per_seq=tiles_per_seq, seq_len=seq_len, aliased=aliased)
    seq_state_spec = pl.BlockSpec((None, seq_per_step, None, LANES, LANES),
                                  lambda j, s: (layer, prompt_tile_of(s), j, 0, 0))
    in_specs = (
        [pl.BlockSpec((n_meta, D), lambda j, s: (0, 0)),
         pl.BlockSpec((ROW_TILE, D), lambda j, s: (next_tile_of(s), 0)),
         pl.BlockSpec((ROW_TILE, D), lambda j, s: (n_p, 0))]
        + [pl.BlockSpec((None, D, LANES), lambda j, s, g=g: (layer, 0, g * H + weights_head(j, s)))
           for g in range(N_GROUPS)]
        + [pl.BlockSpec((None, LANES, D), lambda j, s: (layer, weights_head(j, s), 0)),
           pl.BlockSpec((None, LANES, D), lambda j, s: (layer, H + weights_head(j, s), 0))]
        + ([pl.BlockSpec(memory_space=pl.ANY)] if aliased else [])
        + [pl.BlockSpec((None, head_params.shape[1], LANES), lambda j, s: (layer, 0, j)),
           seq_state_spec,
           pl.BlockSpec((None, ROW_TILE, LANES), lambda j, s: (layer, 0, j))])
    out_specs = [
        pl.BlockSpec((ROW_TILE, 2 * LANES), lambda j, s: (out_tile_of(s), j)),
        pl.BlockSpec((n_meta, 2 * LANES), lambda j, s: (0, j)),
        pl.BlockSpec((None, None, LANES, LANES), lambda j, s: (prompt_tile_of(s), j, 0, 0)),
        pl.BlockSpec((None, SUBLANES, LANES), lambda j, s: (prompt_tile_of(s), 0, j)),
        seq_state_spec,
        pl.BlockSpec((ROW_TILE, LANES), lambda j, s: (0, j)),
        pl.BlockSpec((2 * LANES, D), lambda j, s: (j, 0)),
    ]
    out_shape = [
        jax.ShapeDtypeStruct((R, 2 * W), _BF16),
        jax.ShapeDtypeStruct((n_meta, 2 * W), _BF16),
        jax.ShapeDtypeStruct((n_p, H, LANES, LANES), _F32),
        jax.ShapeDtypeStruct((n_p, SUBLANES, W), _F32),
        jax.ShapeDtypeStruct(state.shape, _F32),
        jax.ShapeDtypeStruct((ROW_TILE, W), _F32),
        jax.ShapeDtypeStruct((2 * W, D), _BF16),
    ]
    scratch = [
        pltpu.VMEM((D, N_GROUPS * LANES), _BF16),
        pltpu.VMEM((ROW_TILE, N_GROUPS * LANES), _F32),
        pltpu.VMEM((ROW_TILE, N_GROUPS * LANES), _F32),
        pltpu.VMEM((n_meta, N_GROUPS * LANES), _F32),
        pltpu.VMEM((ROW_TILE, N_GROUPS * LANES), _F32),
        pltpu.VMEM((LANES, LANES), _F32),
        pltpu.VMEM((SUBLANES, LANES), _F32),
        pltpu.VMEM((LANES, LANES), _F32),
        pltpu.VMEM((SUBLANES, LANES), _F32),
        pltpu.VMEM((PROMPT_CHUNK, LANES), _F32),
    ] + [pltpu.VMEM((ROW_TILE, LANES), _F32)] * 6 + [
        pltpu.VMEM((ROW_TILE, LANES), _BF16)]
    args = ([hnm, hn, hn] + [w_in] * N_GROUPS + [w_out, w_out] + ([prev_states] if aliased else [])
            + [head_params, state, cpad])
    return pl.pallas_call(
        kern, grid=(H, n_steps), in_specs=in_specs, out_specs=out_specs, out_shape=out_shape,
        scratch_shapes=scratch,
        input_output_aliases={N_GROUPS + 5: 4} if aliased else {},
        compiler_params=pltpu.CompilerParams(dimension_semantics=("arbitrary", "arbitrary"),
                                             vmem_limit_bytes=VMEM_LIMIT),
        name="mixer",
    )(*args)


def _out_kernel(*refs, n_prompt_tiles, first, last):
    refs = list(refs)
    if first:
        hp_ref, hs_ref = refs[:2]
        refs = refs[2:]
    else:
        h_ref = refs.pop(0)
    hm_ref, oy_ref, oym_ref, wo_ref, post_ref = refs[:5]
    refs = refs[5:]
    if last:
        yp_ref, ys_ref = refs
    else:
        pre_ref, hout_ref, hn_ref, hmout_ref, hnm_ref = refs
    i = pl.program_id(0)

    def mixed(h, oy):
        mix = jnp.dot(oy, wo_ref[...], preferred_element_type=_F32)
        return h + mix * _rms_scale(mix) * post_ref[...]

    if not last:
        @pl.when(i == 0)
        def _():
            hm = mixed(hm_ref[...], oym_ref[...])
            hmout_ref[...] = hm
            hnm_ref[...] = (hm * _rms_scale(hm) * pre_ref[...]).astype(_BF16)

    if first:
        h_in = jnp.where(i < n_prompt_tiles, hp_ref[...], hs_ref[...])
    else:
        h_in = h_ref[...]
    h = mixed(h_in, oy_ref[...])
    if last:
        @pl.when(i < n_prompt_tiles)
        def _():
            yp_ref[...] = h

        @pl.when(i >= n_prompt_tiles)
        def _():
            ys_ref[...] = h
    else:
        hout_ref[...] = h
        hn_ref[...] = (h * _rms_scale(h) * pre_ref[...]).astype(_BF16)


def _out(h_args, hm, oy, oym, wo16, post_w, pre_w, *, layer, first, last):
    R, W2 = oy.shape
    D = wo16.shape[1]
    n_meta = oym.shape[0]
    n_p = R // ROW_TILE - 1
    kern = functools.partial(_out_kernel, n_prompt_tiles=n_p, first=first, last=last)
    tile_spec = pl.BlockSpec((ROW_TILE, D), lambda i: (i, 0))
    prompt_spec = pl.BlockSpec((ROW_TILE, D), lambda i: (jnp.minimum(i, n_p - 1), 0))
    sample_spec = pl.BlockSpec((ROW_TILE, D), lambda i: (0, 0))
    meta_spec = pl.BlockSpec((n_meta, D), lambda i: (0, 0))
    in_specs = ([prompt_spec, sample_spec] if first else [tile_spec]) + [
        meta_spec,
        pl.BlockSpec((ROW_TILE, W2), lambda i: (i, 0)),
        pl.BlockSpec((n_meta, W2), lambda i: (0, 0)),
        pl.BlockSpec((W2, D), lambda i: (0, 0), pipeline_mode=pl.Buffered(1)),
        pl.BlockSpec((None, 1, D), lambda i: (layer, 0, 0)),
    ]
    args = list(h_args) + [hm, oy, oym, wo16, post_w]
    if last:
        out_specs = [prompt_spec, sample_spec]
        out_shape = [jax.ShapeDtypeStruct((n_p * ROW_TILE, D), _F32),
                     jax.ShapeDtypeStruct((ROW_TILE, D), _F32)]
    else:
        in_specs.append(pl.BlockSpec((None, 1, D), lambda i: (layer + 1, 0, 0)))
        args.append(pre_w)
        out_specs = [tile_spec, tile_spec, meta_spec, meta_spec]
        out_shape = [jax.ShapeDtypeStruct((R, D), _F32), jax.ShapeDtypeStruct((R, D), _BF16),
                     jax.ShapeDtypeStruct((n_meta, D), _F32), jax.ShapeDtypeStruct((n_meta, D), _BF16)]
    return pl.pallas_call(
        kern, grid=(n_p + 1,), in_specs=in_specs, out_specs=out_specs, out_shape=out_shape,
        compiler_params=pltpu.CompilerParams(dimension_semantics=("arbitrary",),
                                             vmem_limit_bytes=VMEM_LIMIT),
        name="out_proj",
    )(*args)


def kernel(x_prompt, x_sample, state_hgrn, state_conv, meta_tokens, w_in, conv_w, lb_param,
           hgrn_norm_w, w_out, pre_norm_w, post_norm_w):
    B, T, D = x_prompt.shape
    DB, DT, _ = x_sample.shape
    depth = w_in.shape[0]
    n_meta = meta_tokens.shape[0]
    W = D // 2
    H = W // LANES
    assert W % LANES == 0 and SUBLANES % DT == 0 and n_meta % (2 * SUBLANES) == 0
    assert T % ROW_TILE == 0 and DB * DT == ROW_TILE and n_meta <= PROMPT_CHUNK
    assert w_in.shape == (depth, D, N_GROUPS * W) and state_conv.shape[2] == CONV_W - 1

    pre = pre_norm_w.reshape(depth, 1, D)
    post = post_norm_w.reshape(depth, 1, D)
    n_rows = depth + CONV_W + 1
    head_params = jnp.concatenate(
        [jnp.broadcast_to(lb_param[None], (depth, depth, W)), conv_w,
         jnp.tile(hgrn_norm_w, (1, H))[:, None, :],
         jnp.zeros((depth, -n_rows % SUBLANES, W), _F32)], axis=1)
    xp = x_prompt.reshape(B * T, D)
    xs = x_sample.reshape(DB * DT, D)
    cpad = jnp.concatenate(
        [state_conv, jnp.zeros((depth, DB, DT - (CONV_W - 1), W), _F32)], axis=2
    ).reshape(depth, DB * DT, W)
    tiles_per_seq = T // ROW_TILE
    last_tiles = slice(tiles_per_seq - 1, None, tiles_per_seq)

    hn, hnm = _first_norm(xp, xs, meta_tokens, pre)
    h_args, hm = (xp, xs), meta_tokens
    s_sample = None
    s_list, c_list, u_list = [], [], []
    for layer in range(depth):
        oy, oym, s_p, c_p, s_sample, u_s, wo16 = _mixer(
            hnm, hn, w_in, w_out, head_params, state_hgrn, cpad, s_sample,
            layer=layer, tiles_per_seq=tiles_per_seq, seq_len=DT)
        s_list.append(s_p[last_tiles])
        c_list.append(c_p[last_tiles, SUBLANES - (CONV_W - 1):])
        u_list.append(u_s.reshape(DB, DT, W)[:, DT - (CONV_W - 1):])
        last = layer + 1 == depth
        outs = _out(h_args, hm, oy, oym, wo16, post, pre, layer=layer, first=layer == 0, last=last)
        if not last:
            h_all, hn, hm, hnm = outs
            h_args = (h_all,)
    y_prompt = outs[0].reshape(B, T, D)
    y_sample = outs[1].reshape(DB, DT, D)
    return (y_prompt, y_sample, jnp.stack(s_list), jnp.stack(c_list), s_sample, jnp.stack(u_list))
```

```python
import functools

import jax
import jax.numpy as jnp
from jax import lax
from jax.experimental import pallas as pl
from jax.experimental.pallas import tpu as pltpu

LANES = 128
SUBLANES = 8
N_GROUPS = 8
CONV_W = 3
EPS = 1e-6
F_FLOOR = 1e-30
LOG2_E = 1.4426950408889634
ROW_TILE = 512
MIX_TILE = 1024
NEXT_COL_PIECES = 4
PROMPT_CHUNK = 128
VMEM_LIMIT = 56 * 1024 * 1024

_F32 = jnp.float32
_BF16 = jnp.bfloat16
_NT = (((1,), (1,)), ((), ()))
_O_COLS = slice(0, LANES)
_YB_COLS = slice(LANES, 2 * LANES)


def _sigmoid(x):
    return 1.0 / (1.0 + jnp.exp(-x))


def _silu(x):
    return x * _sigmoid(x)


def _rms_scale(x):
    return lax.rsqrt(jnp.mean(x * x, axis=-1, keepdims=True) + EPS)


def _lower_bound(lbp, layer):
    e = jnp.exp(lbp - jnp.max(lbp, axis=0, keepdims=True))
    w = e / jnp.sum(e, axis=0, keepdims=True)
    r = lax.broadcasted_iota(jnp.int32, w.shape, 0)
    return jnp.sum(jnp.where((r >= 1) & (r <= layer), w, 0.0), axis=0, keepdims=True)


def _forget_gates(fz, lb):
    one_m = 1.0 - lb
    f = lb + one_m * _sigmoid(fz)
    return jnp.log(jnp.maximum(f, F_FLOOR)), one_m * _sigmoid(-fz)


def _shift_rows(x, s):
    return pltpu.roll(x, s % x.shape[0], 0)


def _cumsum_rows(x, row):
    s = 1
    while s < x.shape[0]:
        x = x + jnp.where(row >= s, _shift_rows(x, s), 0.0)
        s *= 2
    return x


def _anchor_rows(g_ref, G, n, row):
    C = G.shape[0]
    half = n // 2
    if n == 2:
        return jnp.where((row & 1) == 1, _shift_rows(G, 1), G)
    if n == 4:
        pos = row & 3
        return jnp.where(pos == 0, _shift_rows(G, -1),
                         jnp.where(pos == 1, G,
                                   jnp.where(pos == 2, _shift_rows(G, 1), _shift_rows(G, 2))))
    parts = []
    for a in range(C // n):
        r = g_ref[pl.ds(a * n + half - 1, 1), :]
        parts.append(jnp.broadcast_to(r, (n, LANES)))
    return parts[0] if len(parts) == 1 else jnp.concatenate(parts, axis=0)


def _intra_chunk_scores(q, kk, G2, g_ref, row, col):
    C = q.shape[0]
    A = jnp.where(row == col, jnp.sum(q * kk, axis=-1, keepdims=True), 0.0)
    q16 = q.astype(_BF16)
    k16 = kk.astype(_BF16)
    n = 2
    while n <= C:
        half = n // 2
        E = jnp.exp2(-jnp.abs(G2 - _anchor_rows(g_ref, G2, n, row))).astype(_BF16)
        P = lax.dot_general(q16 * E, k16 * E, _NT, preferred_element_type=_F32)
        shift = n.bit_length() - 1
        mask = ((row >> shift) == (col >> shift)) & ((row & (n - 1)) >= half) & ((col & (n - 1)) < half)
        A = jnp.where(mask, P, A)
        n *= 2
    return A


def _group(ref, g, rows=slice(None)):
    return ref[rows, g * LANES:(g + 1) * LANES]


def _conv_gate(bg, zb, u, um1, um2, cw):
    c = cw[0:1] * um2 + cw[1:2] * um1 + cw[2:3] * u
    return bg * c * _silu(zb)


def _chunked_head_mixer(proj_ref, lb, nw, cw, st_ref, cb_ref, g_ref, chunk, oy_ref, filler=None):
    TM = proj_ref.shape[0]
    C = chunk
    n_chunks = TM // C
    row = lax.broadcasted_iota(jnp.int32, (C, 1), 0)
    col = lax.broadcasted_iota(jnp.int32, (1, C), 1)

    def scores(c):
        rows = slice(c * C, (c + 1) * C)
        q = _group(proj_ref, 0, rows)
        v16 = _group(proj_ref, 2, rows).astype(_BF16)
        g, kk = _forget_gates(_group(proj_ref, 1, rows), lb)
        G2 = _cumsum_rows(g * LOG2_E, row)
        g_ref[0:C] = G2
        A = _intra_chunk_scores(q, kk, G2, g_ref, row, col).astype(_BF16)
        GL2 = G2[C - 1:C]
        return (A, v16, (q * jnp.exp2(G2)).astype(_BF16), (kk * jnp.exp2(GL2 - G2)).astype(_BF16),
                jnp.exp2(GL2))

    def advance(c, staged):
        A, v16, qdec, khat, decay = staged
        rows = slice(c * C, (c + 1) * C)
        St = st_ref[...]
        o = jnp.dot(A, v16, preferred_element_type=_F32)
        o = o + lax.dot_general(qdec, St.astype(_BF16), _NT, preferred_element_type=_F32)
        st_ref[...] = St * decay + pl.dot(v16, khat, trans_a=True)
        o = o * _rms_scale(o) * nw * _silu(_group(proj_ref, 3, rows))
        oy_ref[rows, _O_COLS] = o.astype(oy_ref.dtype)

    for c in range(n_chunks):
        if filler is not None:
            filler(c, n_chunks)
        advance(c, scores(c))

    u = _group(proj_ref, 5) * _group(proj_ref, 6)
    prev = cb_ref[...]
    trow = lax.broadcasted_iota(jnp.int32, (TM, 1), 0)
    um1 = jnp.where(trow >= 1, _shift_rows(u, 1), prev[7:8])
    um2 = jnp.where(trow >= 2, _shift_rows(u, 2), jnp.where(trow == 0, prev[6:7], prev[7:8]))
    oy_ref[:, _YB_COLS] = _conv_gate(_group(proj_ref, 4), _group(proj_ref, 7), u, um1, um2,
                                     cw).astype(oy_ref.dtype)
    cb_ref[...] = u[TM - SUBLANES:TM]


def _sample_prepare(proj_ref, lb, cw, cpad_ref, u_ref, qt_ref, kh_ref, dec_ref, v_ref, os_ref, ybs_ref, seq_len):
    T = seq_len
    TM = proj_ref.shape[0]
    row = lax.broadcasted_iota(jnp.int32, (TM, 1), 0)
    pos = row % T
    q = _group(proj_ref, 0)
    v = _group(proj_ref, 2)
    g, kk = _forget_gates(_group(proj_ref, 1), lb)

    G = g
    for d in range(1, T):
        G = G + jnp.where(pos >= d, _shift_rows(g, d), 0.0)
    o = jnp.sum(q * kk, axis=-1, keepdims=True) * v
    for d in range(1, T):
        decay = jnp.exp(jnp.minimum(G - _shift_rows(G, d), 0.0))
        a = jnp.sum(jnp.where(pos >= d, q * _shift_rows(kk, d) * decay, 0.0), axis=-1, keepdims=True)
        o = o + a * _shift_rows(v, d)
    GL = G
    for d in range(1, T):
        GL = jnp.where(pos == T - 1 - d, _shift_rows(G, -d), GL)
    qt_ref[...] = q * jnp.exp(G)
    kh_ref[...] = kk * jnp.exp(GL - G)
    dec_ref[...] = jnp.exp(GL)
    v_ref[...] = v
    os_ref[...] = o

    u = _group(proj_ref, 5) * _group(proj_ref, 6)
    e = cpad_ref[...]
    um1 = jnp.where(pos >= 1, _shift_rows(u, 1), _shift_rows(e, -1))
    um2 = jnp.where(pos >= 2, _shift_rows(u, 2), e)
    ybs_ref[...] = _conv_gate(_group(proj_ref, 4), _group(proj_ref, 7), u, um1, um2, cw).astype(ybs_ref.dtype)
    u_ref[...] = u


def _sample_group(group, local_group, sin_ref, sout_ref, qt_ref, kh_ref, dec_ref, v_ref, oi_ref, seq_len):
    T = seq_len
    per_group = SUBLANES // T
    row8 = lax.broadcasted_iota(jnp.int32, (SUBLANES, 1), 0)
    r = pl.multiple_of(group * SUBLANES, SUBLANES)
    q8 = qt_ref[pl.ds(r, SUBLANES), :].astype(_BF16)
    k8 = kh_ref[pl.ds(r, SUBLANES), :]
    v8 = v_ref[pl.ds(r, SUBLANES), :].astype(_BF16)
    d8 = dec_ref[pl.ds(r, SUBLANES), :]
    o8 = jnp.zeros((SUBLANES, LANES), _F32)
    for i in range(per_group):
        b = local_group * per_group + i
        mine = (row8 >= i * T) & (row8 < (i + 1) * T)
        S0 = sin_ref[b]
        o8 = jnp.where(mine, jnp.dot(q8, S0.astype(_BF16), preferred_element_type=_F32), o8)
        upd = pl.dot(jnp.where(mine, k8, 0.0).astype(_BF16), v8, trans_a=True)
        decay = jnp.broadcast_to(d8[i * T:i * T + 1], (LANES, LANES)).T
        sout_ref[b] = decay * S0 + upd
    oi_ref[pl.ds(r, SUBLANES), :] = o8


def _sample_finish(proj_ref, nw, os_ref, oi_ref, ybs_ref, oy_ref):
    o = os_ref[...] + oi_ref[...]
    o = o * _rms_scale(o) * nw * _silu(_group(proj_ref, 3))
    oy_ref[:, _O_COLS] = o.astype(oy_ref.dtype)
    oy_ref[:, _YB_COLS] = ybs_ref[...]


def _first_norm_kernel(xp_ref, xs_ref, xm_ref, w_ref, hn_ref, hnm_ref, *, n_prompt_tiles):
    i = pl.program_id(0)
    x = jnp.where(i < n_prompt_tiles, xp_ref[...], xs_ref[...])
    hn_ref[...] = (x * _rms_scale(x) * w_ref[...]).astype(_BF16)

    @pl.when(i == 0)
    def _():
        xm = xm_ref[...]
        hnm_ref[...] = (xm * _rms_scale(xm) * w_ref[...]).astype(_BF16)


def _first_norm(xp, xs, xm, pre_w):
    Rp, D = xp.shape
    n_p = Rp // ROW_TILE
    n_meta = xm.shape[0]
    return pl.pallas_call(
        functools.partial(_first_norm_kernel, n_prompt_tiles=n_p), grid=(n_p + 1,),
        in_specs=[pl.BlockSpec((ROW_TILE, D), lambda i: (jnp.minimum(i, n_p - 1), 0)),
                  pl.BlockSpec((ROW_TILE, D), lambda i: (0, 0)),
                  pl.BlockSpec((n_meta, D), lambda i: (0, 0)),
                  pl.BlockSpec((None, 1, D), lambda i: (0, 0, 0))],
        out_specs=[pl.BlockSpec((ROW_TILE, D), lambda i: (i, 0)),
                   pl.BlockSpec((n_meta, D), lambda i: (0, 0))],
        out_shape=[jax.ShapeDtypeStruct((Rp + ROW_TILE, D), _BF16),
                   jax.ShapeDtypeStruct((n_meta, D), _BF16)],
        compiler_params=pltpu.CompilerParams(dimension_semantics=("arbitrary",),
                                             vmem_limit_bytes=VMEM_LIMIT),
        name="first_norm",
    )(xp, xs, xm, pre_w)


def _mixer_kernel(hnm_ref, hn_ref, hns_ref, *refs, layer, depth, n_prompt_tiles, tiles_per_seq, seq_len,
                  aliased):
    slabs = refs[:N_GROUPS]
    woa_ref, wob_ref = refs[N_GROUPS:N_GROUPS + 2]
    refs = refs[N_GROUPS + 2:]
    if aliased:
        refs = refs[1:]
    (prm_ref, sin_ref, cpad_ref,
     oy_ref, oys_ref, oym_ref, sp_ref, cp_ref, ss_ref, us_ref, wo16_ref,
     w_scr, proj_ref, next_ref, projm_ref, projs_ref, st_ref, cb_ref, st0_ref, cb0_ref, g_ref,
     qt_ref, kh_ref, dec_ref, v_ref, os_ref, oi_ref, ybs_ref) = refs
    s = pl.program_id(1)
    n_meta = hnm_ref.shape[0]
    groups_per_step = sin_ref.shape[0] * seq_len // SUBLANES
    lb = _lower_bound(prm_ref[0:depth], layer)
    cw = prm_ref[depth:depth + CONV_W]
    nw = prm_ref[depth + CONV_W:depth + CONV_W + 1]

    @pl.when(s == 0)
    def _():
        for g, slab in enumerate(slabs):
            w_scr[:, g * LANES:(g + 1) * LANES] = slab[...].astype(_BF16)
        wo16_ref[_O_COLS.start:_O_COLS.stop] = woa_ref[...].astype(_BF16)
        wo16_ref[_YB_COLS.start:_YB_COLS.stop] = wob_ref[...].astype(_BF16)
        projm_ref[...] = jnp.dot(hnm_ref[...], w_scr[...], preferred_element_type=_F32)
        st_ref[...] = jnp.zeros_like(st_ref)
        cb_ref[...] = jnp.zeros_like(cb_ref)
        _chunked_head_mixer(projm_ref, lb, nw, cw, st_ref, cb_ref, g_ref, n_meta, oym_ref)
        st0_ref[...] = st_ref[...]
        cb0_ref[...] = cb_ref[...]
        projs_ref[...] = jnp.dot(hns_ref[...], w_scr[...], preferred_element_type=_F32)
        _sample_prepare(projs_ref, lb, cw, cpad_ref, us_ref, qt_ref, kh_ref, dec_ref, v_ref, os_ref, ybs_ref,
                        seq_len)
        proj_ref[...] = jnp.dot(hn_ref[...], w_scr[...], preferred_element_type=_F32)

    def prompt_step(project_next):
        @pl.when((s - 1) % tiles_per_seq == 0)
        def _():
            st_ref[...] = st0_ref[...]
            cb_ref[...] = cb0_ref[...]

        def filler(c, n):
            if project_next:
                n_cols = N_GROUPS * LANES // NEXT_COL_PIECES
                n_rows = MIX_TILE * NEXT_COL_PIECES // n
                rows = slice((c // NEXT_COL_PIECES) * n_rows, (c // NEXT_COL_PIECES + 1) * n_rows)
                cols = slice((c % NEXT_COL_PIECES) * n_cols, (c % NEXT_COL_PIECES + 1) * n_cols)
                next_ref[rows, cols] = jnp.dot(hn_ref[rows, :], w_scr[:, cols], preferred_element_type=_F32)
            for i in range(c * groups_per_step // n, (c + 1) * groups_per_step // n):
                _sample_group((s - 1) * groups_per_step + i, i, sin_ref, ss_ref, qt_ref, kh_ref, dec_ref,
                              v_ref, oi_ref, seq_len)

        _chunked_head_mixer(proj_ref, lb, nw, cw, st_ref, cb_ref, g_ref, PROMPT_CHUNK, oy_ref, filler=filler)
        cp_ref[...] = cb_ref[...]
        sp_ref[...] = st_ref[...].T
        if project_next:
            proj_ref[...] = next_ref[...]

    @pl.when((s >= 1) & (s < n_prompt_tiles))
    def _():
        prompt_step(True)

    @pl.when(s == n_prompt_tiles)
    def _():
        prompt_step(False)

    @pl.when(s > n_prompt_tiles)
    def _():
        _sample_finish(projs_ref, nw, os_ref, oi_ref, ybs_ref, oys_ref)


def _mixer(hnm, hn, w_in, w_out, head_params, state, cpad, prev_states, *, layer, tiles_per_seq, seq_len):
    R, D = hn.shape
    n_meta = hnm.shape[0]
    depth = w_in.shape[0]
    W = w_in.shape[2] // N_GROUPS
    H = W // LANES
    n_p = (R - ROW_TILE) // MIX_TILE
    n_seq = state.shape[1]
    seq_per_step = n_seq // n_p
    assert seq_per_step * n_p == n_seq and (seq_per_step * seq_len) % (SUBLANES * (MIX_TILE // PROMPT_CHUNK)) == 0
    n_steps = n_p + 2
    aliased = prev_states is not None

    def weights_head(j, s):
        return jnp.minimum(j + (s >= 2).astype(jnp.int32), H - 1)

    def next_tile_of(s):
        return jnp.clip(s, 0, n_p - 1)

    def prompt_tile_of(s):
        return jnp.clip(s - 1, 0, n_p - 1)

    kern = functools.partial(_mixer_kernel, layer=layer, depth=depth, n_prompt_tiles=n_p,
                             tiles_per_seq=tiles_per_seq, seq_len=seq_len, aliased=aliased)
    seq_state_spec = pl.BlockSpec((None, seq_per_step, None, LANES, LANES),
                                  lambda j, s: (layer, prompt_tile_of(s), j, 0, 0))
    in_specs = (
        [pl.BlockSpec((n_meta, D), lambda j, s: (0, 0)),
         pl.BlockSpec((MIX_TILE, D), lambda j, s: (next_tile_of(s), 0)),
         pl.BlockSpec((ROW_TILE, D), lambda j, s: (R // ROW_TILE - 1, 0))]
        + [pl.BlockSpec((None, D, LANES), lambda j, s, g=g: (layer, 0, g * H + weights_head(j, s)))
           for g in range(N_GROUPS)]
        + [pl.BlockSpec((None, LANES, D), lambda j, s: (layer, weights_head(j, s), 0)),
           pl.BlockSpec((None, LANES, D), lambda j, s: (layer, H + weights_head(j, s), 0))]
        + ([pl.BlockSpec(memory_space=pl.ANY)] if aliased else [])
        + [pl.BlockSpec((None, head_params.shape[1], LANES), lambda j, s: (layer, 0, j)),
           seq_state_spec,
           pl.BlockSpec((None, ROW_TILE, LANES), lambda j, s: (layer, 0, j))])
    out_specs = [
        pl.BlockSpec((MIX_TILE, 2 * LANES), lambda j, s: (prompt_tile_of(s), j)),
        pl.BlockSpec((ROW_TILE, 2 * LANES), lambda j, s: (0, j)),
        pl.BlockSpec((n_meta, 2 * LANES), lambda j, s: (0, j)),
        pl.BlockSpec((None, None, LANES, LANES), lambda j, s: (prompt_tile_of(s), j, 0, 0)),
        pl.BlockSpec((None, SUBLANES, LANES), lambda j, s: (prompt_tile_of(s), 0, j)),
        seq_state_spec,
        pl.BlockSpec((ROW_TILE, LANES), lambda j, s: (0, j)),
        pl.BlockSpec((2 * LANES, D), lambda j, s: (j, 0)),
    ]
    out_shape = [
        jax.ShapeDtypeStruct((R - ROW_TILE, 2 * W), _BF16),
        jax.ShapeDtypeStruct((ROW_TILE, 2 * W), _BF16),
        jax.ShapeDtypeStruct((n_meta, 2 * W), _BF16),
        jax.ShapeDtypeStruct((n_p, H, LANES, LANES), _F32),
        jax.ShapeDtypeStruct((n_p, SUBLANES, W), _F32),
        jax.ShapeDtypeStruct(state.shape, _F32),
        jax.ShapeDtypeStruct((ROW_TILE, W), _F32),
        jax.ShapeDtypeStruct((2 * W, D), _BF16),
    ]
    scratch = [
        pltpu.VMEM((D, N_GROUPS * LANES), _BF16),
        pltpu.VMEM((MIX_TILE, N_GROUPS * LANES), _F32),
        pltpu.VMEM((MIX_TILE, N_GROUPS * LANES), _F32),
        pltpu.VMEM((n_meta, N_GROUPS * LANES), _F32),
        pltpu.VMEM((ROW_TILE, N_GROUPS * LANES), _F32),
        pltpu.VMEM((LANES, LANES), _F32),
        pltpu.VMEM((SUBLANES, LANES), _F32),
        pltpu.VMEM((LANES, LANES), _F32),
        pltpu.VMEM((SUBLANES, LANES), _F32),
        pltpu.VMEM((PROMPT_CHUNK, LANES), _F32),
    ] + [pltpu.VMEM((ROW_TILE, LANES), _F32)] * 6 + [
        pltpu.VMEM((ROW_TILE, LANES), _BF16)]
    args = ([hnm, hn, hn] + [w_in] * N_GROUPS + [w_out, w_out] + ([prev_states] if aliased else [])
            + [head_params, state, cpad])
    return pl.pallas_call(
        kern, grid=(H, n_steps), in_specs=in_specs, out_specs=out_specs, out_shape=out_shape,
        scratch_shapes=scratch,
        input_output_aliases={N_GROUPS + 5: 5} if aliased else {},
        compiler_params=pltpu.CompilerParams(dimension_semantics=("arbitrary", "arbitrary"),
                                             vmem_limit_bytes=VMEM_LIMIT),
        name="mixer",
    )(*args)


def _out_kernel(*refs, n_prompt_tiles, first, last):
    refs = list(refs)
    if first:
        hp_ref, hs_ref = refs[:2]
        refs = refs[2:]
    else:
        h_ref = refs.pop(0)
    hm_ref, oy_ref, oys_ref, oym_ref, wo_ref, post_ref = refs[:6]
    refs = refs[6:]
    if last:
        yp_ref, ys_ref = refs
    else:
        pre_ref, hout_ref, hn_ref, hmout_ref, hnm_ref = refs
    i = pl.program_id(0)

    def mixed(h, oy):
        mix = jnp.dot(oy, wo_ref[...], preferred_element_type=_F32)
        return h + mix * _rms_scale(mix) * post_ref[...]

    if not last:
        @pl.when(i == 0)
        def _():
            hm = mixed(hm_ref[...], oym_ref[...])
            hmout_ref[...] = hm
            hnm_ref[...] = (hm * _rms_scale(hm) * pre_ref[...]).astype(_BF16)

    if first:
        h_in = jnp.where(i < n_prompt_tiles, hp_ref[...], hs_ref[...])
    else:
        h_in = h_ref[...]
    h = mixed(h_in, jnp.where(i < n_prompt_tiles, oy_ref[...], oys_ref[...]))
    if last:
        @pl.when(i < n_prompt_tiles)
        def _():
            yp_ref[...] = h

        @pl.when(i >= n_prompt_tiles)
        def _():
            ys_ref[...] = h
    else:
        hout_ref[...] = h
        hn_ref[...] = (h * _rms_scale(h) * pre_ref[...]).astype(_BF16)


def _out(h_args, hm, oy, oys, oym, wo16, post_w, pre_w, *, layer, first, last):
    Rp, W2 = oy.shape
    R = Rp + oys.shape[0]
    D = wo16.shape[1]
    n_meta = oym.shape[0]
    n_p = Rp // ROW_TILE
    kern = functools.partial(_out_kernel, n_prompt_tiles=n_p, first=first, last=last)
    tile_spec = pl.BlockSpec((ROW_TILE, D), lambda i: (i, 0))
    prompt_spec = pl.BlockSpec((ROW_TILE, D), lambda i: (jnp.minimum(i, n_p - 1), 0))
    sample_spec = pl.BlockSpec((ROW_TILE, D), lambda i: (0, 0))
    meta_spec = pl.BlockSpec((n_meta, D), lambda i: (0, 0))
    in_specs = ([prompt_spec, sample_spec] if first else [tile_spec]) + [
        meta_spec,
        pl.BlockSpec((ROW_TILE, W2), lambda i: (jnp.minimum(i, n_p - 1), 0)),
        pl.BlockSpec((ROW_TILE, W2), lambda i: (0, 0)),
        pl.BlockSpec((n_meta, W2), lambda i: (0, 0)),
        pl.BlockSpec((W2, D), lambda i: (0, 0), pipeline_mode=pl.Buffered(1)),
        pl.BlockSpec((None, 1, D), lambda i: (layer, 0, 0)),
    ]
    args = list(h_args) + [hm, oy, oys, oym, wo16, post_w]
    if last:
        out_specs = [prompt_spec, sample_spec]
        out_shape = [jax.ShapeDtypeStruct((n_p * ROW_TILE, D), _F32),
                     jax.ShapeDtypeStruct((ROW_TILE, D), _F32)]
    else:
        in_specs.append(pl.BlockSpec((None, 1, D), lambda i: (layer + 1, 0, 0)))
        args.append(pre_w)
        out_specs = [tile_spec, tile_spec, meta_spec, meta_spec]
        out_shape = [jax.ShapeDtypeStruct((R, D), _F32), jax.ShapeDtypeStruct((R, D), _BF16),
                     jax.ShapeDtypeStruct((n_meta, D), _F32), jax.ShapeDtypeStruct((n_meta, D), _BF16)]
    return pl.pallas_call(
        kern, grid=(n_p + 1,), in_specs=in_specs, out_specs=out_specs, out_shape=out_shape,
        compiler_params=pltpu.CompilerParams(dimension_semantics=("arbitrary",),
                                             vmem_limit_bytes=VMEM_LIMIT),
        name="out_proj",
    )(*args)


def kernel(x_prompt, x_sample, state_hgrn, state_conv, meta_tokens, w_in, conv_w, lb_param,
           hgrn_norm_w, w_out, pre_norm_w, post_norm_w):
    B, T, D = x_prompt.shape
    DB, DT, _ = x_sample.shape
    depth = w_in.shape[0]
    n_meta = meta_tokens.shape[0]
    W = D // 2
    H = W // LANES
    assert W % LANES == 0 and SUBLANES % DT == 0 and n_meta % (2 * SUBLANES) == 0
    assert T % MIX_TILE == 0 and DB * DT == ROW_TILE and n_meta <= PROMPT_CHUNK
    assert w_in.shape == (depth, D, N_GROUPS * W) and state_conv.shape[2] == CONV_W - 1

    pre = pre_norm_w.reshape(depth, 1, D)
    post = post_norm_w.reshape(depth, 1, D)
    n_rows = depth + CONV_W + 1
    head_params = jnp.concatenate(
        [jnp.broadcast_to(lb_param[None], (depth, depth, W)), conv_w,
         jnp.tile(hgrn_norm_w, (1, H))[:, None, :],
         jnp.zeros((depth, -n_rows % SUBLANES, W), _F32)], axis=1)
    xp = x_prompt.reshape(B * T, D)
    xs = x_sample.reshape(DB * DT, D)
    cpad = jnp.concatenate(
        [state_conv, jnp.zeros((depth, DB, DT - (CONV_W - 1), W), _F32)], axis=2
    ).reshape(depth, DB * DT, W)
    tiles_per_seq = T // MIX_TILE
    last_tiles = slice(tiles_per_seq - 1, None, tiles_per_seq)

    hn, hnm = _first_norm(xp, xs, meta_tokens, pre)
    h_args, hm = (xp, xs), meta_tokens
    s_sample = None
    s_list, c_list, u_list = [], [], []
    for layer in range(depth):
        oy, oys, oym, s_p, c_p, s_sample, u_s, wo16 = _mixer(
            hnm, hn, w_in, w_out, head_params, state_hgrn, cpad, s_sample,
            layer=layer, tiles_per_seq=tiles_per_seq, seq_len=DT)
        s_list.append(s_p[last_tiles])
        c_list.append(c_p[last_tiles, SUBLANES - (CONV_W - 1):])
        u_list.append(u_s.reshape(DB, DT, W)[:, DT - (CONV_W - 1):])
        last = layer + 1 == depth
        outs = _out(h_args, hm, oy, oys, oym, wo16, post, pre, layer=layer, first=layer == 0, last=last)
        if not last:
            h_all, hn, hm, hnm = outs
            h_args = (h_all,)
    y_prompt = outs[0].reshape(B, T, D)
    y_sample = outs[1].reshape(DB, DT, D)
    return (y_prompt, y_sample, jnp.stack(s_list), jnp.stack(c_list), s_sample, jnp.stack(u_list))
```

```python
import functools

import jax
import jax.numpy as jnp
from jax import lax
from jax.experimental import pallas as pl
from jax.experimental.pallas import tpu as pltpu

LANES = 128
SUBLANES = 8
N_GROUPS = 8
CONV_W = 3
EPS = 1e-6
F_FLOOR = 1e-30
LOG2_E = 1.4426950408889634
ROW_TILE = 512
MIX_TILE = 1024
NEXT_COL_PIECES = 4
PROMPT_CHUNK = 128
VMEM_LIMIT = 56 * 1024 * 1024

_F32 = jnp.float32
_BF16 = jnp.bfloat16
_NT = (((1,), (1,)), ((), ()))
_O_COLS = slice(0, LANES)
_YB_COLS = slice(LANES, 2 * LANES)


def _sigmoid(x):
    return 1.0 / (1.0 + jnp.exp(-x))


def _silu(x):
    return x * _sigmoid(x)


def _rms_scale(x):
    return lax.rsqrt(jnp.mean(x * x, axis=-1, keepdims=True) + EPS)


def _lower_bound(lbp, layer):
    e = jnp.exp(lbp - jnp.max(lbp, axis=0, keepdims=True))
    w = e / jnp.sum(e, axis=0, keepdims=True)
    r = lax.broadcasted_iota(jnp.int32, w.shape, 0)
    return jnp.sum(jnp.where((r >= 1) & (r <= layer), w, 0.0), axis=0, keepdims=True)


def _forget_gates(fz, lb):
    one_m = 1.0 - lb
    f = lb + one_m * _sigmoid(fz)
    return jnp.log(jnp.maximum(f, F_FLOOR)), one_m * _sigmoid(-fz)


def _shift_rows(x, s):
    return pltpu.roll(x, s % x.shape[0], 0)


def _cumsum_rows(x, row):
    s = 1
    while s < x.shape[0]:
        x = x + jnp.where(row >= s, _shift_rows(x, s), 0.0)
        s *= 2
    return x


def _anchor_rows(g_ref, G, n, row):
    C = G.shape[0]
    half = n // 2
    if n == 2:
        return jnp.where((row & 1) == 1, _shift_rows(G, 1), G)
    if n == 4:
        pos = row & 3
        return jnp.where(pos == 0, _shift_rows(G, -1),
                         jnp.where(pos == 1, G,
                                   jnp.where(pos == 2, _shift_rows(G, 1), _shift_rows(G, 2))))
    parts = []
    for a in range(C // n):
        r = g_ref[pl.ds(a * n + half - 1, 1), :]
        parts.append(jnp.broadcast_to(r, (n, LANES)))
    return parts[0] if len(parts) == 1 else jnp.concatenate(parts, axis=0)


def _intra_chunk_scores(q, kk, G2, g_ref, row, col):
    C = q.shape[0]
    A = jnp.where(row == col, jnp.sum(q * kk, axis=-1, keepdims=True), 0.0)
    q16 = q.astype(_BF16)
    k16 = kk.astype(_BF16)
    n = 2
    while n <= C:
        half = n // 2
        E = jnp.exp2(-jnp.abs(G2 - _anchor_rows(g_ref, G2, n, row))).astype(_BF16)
        P = lax.dot_general(q16 * E, k16 * E, _NT, preferred_element_type=_F32)
        shift = n.bit_length() - 1
        mask = ((row >> shift) == (col >> shift)) & ((row & (n - 1)) >= half) & ((col & (n - 1)) < half)
        A = jnp.where(mask, P, A)
        n *= 2
    return A


def _group(ref, g, rows=slice(None)):
    return ref[rows, g * LANES:(g + 1) * LANES]


def _conv_gate(bg, zb, u, um1, um2, cw):
    c = cw[0:1] * um2 + cw[1:2] * um1 + cw[2:3] * u
    return bg * c * _silu(zb)


def _chunked_head_mixer(proj_ref, lb, nw, cw, st_ref, cb_ref, g_ref, chunk, oy_ref, filler=None):
    TM = proj_ref.shape[0]
    C = chunk
    n_chunks = TM // C
    row = lax.broadcasted_iota(jnp.int32, (C, 1), 0)
    col = lax.broadcasted_iota(jnp.int32, (1, C), 1)

    def scores(c):
        rows = slice(c * C, (c + 1) * C)
        q = _group(proj_ref, 0, rows)
        v16 = _group(proj_ref, 2, rows).astype(_BF16)
        g, kk = _forget_gates(_group(proj_ref, 1, rows), lb)
        G2 = _cumsum_rows(g * LOG2_E, row)
        g_ref[0:C] = G2
        A = _intra_chunk_scores(q, kk, G2, g_ref, row, col).astype(_BF16)
        GL2 = G2[C - 1:C]
        return (A, v16, (q * jnp.exp2(G2)).astype(_BF16), (kk * jnp.exp2(GL2 - G2)).astype(_BF16),
                jnp.exp2(GL2))

    def advance(c, staged):
        A, v16, qdec, khat, decay = staged
        rows = slice(c * C, (c + 1) * C)
        St = st_ref[...]
        o = jnp.dot(A, v16, preferred_element_type=_F32)
        o = o + lax.dot_general(qdec, St.astype(_BF16), _NT, preferred_element_type=_F32)
        st_ref[...] = St * decay + pl.dot(v16, khat, trans_a=True)
        o = o * _rms_scale(o) * nw * _silu(_group(proj_ref, 3, rows))
        oy_ref[rows, _O_COLS] = o.astype(oy_ref.dtype)

    for c in range(n_chunks):
        if filler is not None:
            filler(c, n_chunks)
        advance(c, scores(c))

    u = _group(proj_ref, 5) * _group(proj_ref, 6)
    prev = cb_ref[...]
    trow = lax.broadcasted_iota(jnp.int32, (TM, 1), 0)
    um1 = jnp.where(trow >= 1, _shift_rows(u, 1), prev[7:8])
    um2 = jnp.where(trow >= 2, _shift_rows(u, 2), jnp.where(trow == 0, prev[6:7], prev[7:8]))
    oy_ref[:, _YB_COLS] = _conv_gate(_group(proj_ref, 4), _group(proj_ref, 7), u, um1, um2,
                                     cw).astype(oy_ref.dtype)
    cb_ref[...] = u[TM - SUBLANES:TM]


def _sample_prepare(proj_ref, lb, cw, cpad_ref, u_ref, qt_ref, kh_ref, dec_ref, v_ref, os_ref, ybs_ref, seq_len):
    T = seq_len
    TM = proj_ref.shape[0]
    row = lax.broadcasted_iota(jnp.int32, (TM, 1), 0)
    pos = row % T
    q = _group(proj_ref, 0)
    v = _group(proj_ref, 2)
    g, kk = _forget_gates(_group(proj_ref, 1), lb)

    G = g
    for d in range(1, T):
        G = G + jnp.where(pos >= d, _shift_rows(g, d), 0.0)
    o = jnp.sum(q * kk, axis=-1, keepdims=True) * v
    for d in range(1, T):
        decay = jnp.exp(jnp.minimum(G - _shift_rows(G, d), 0.0))
        a = jnp.sum(jnp.where(pos >= d, q * _shift_rows(kk, d) * decay, 0.0), axis=-1, keepdims=True)
        o = o + a * _shift_rows(v, d)
    GL = G
    for d in range(1, T):
        GL = jnp.where(pos == T - 1 - d, _shift_rows(G, -d), GL)
    qt_ref[...] = q * jnp.exp(G)
    kh_ref[...] = kk * jnp.exp(GL - G)
    dec_ref[...] = jnp.exp(GL)
    v_ref[...] = v
    os_ref[...] = o

    u = _group(proj_ref, 5) * _group(proj_ref, 6)
    e = cpad_ref[...]
    um1 = jnp.where(pos >= 1, _shift_rows(u, 1), _shift_rows(e, -1))
    um2 = jnp.where(pos >= 2, _shift_rows(u, 2), e)
    ybs_ref[...] = _conv_gate(_group(proj_ref, 4), _group(proj_ref, 7), u, um1, um2, cw).astype(ybs_ref.dtype)
    u_ref[...] = u


def _sample_group(group, local_group, sin_ref, sout_ref, qt_ref, kh_ref, dec_ref, v_ref, oi_ref, seq_len):
    T = seq_len
    per_group = SUBLANES // T
    row8 = lax.broadcasted_iota(jnp.int32, (SUBLANES, 1), 0)
    r = pl.multiple_of(group * SUBLANES, SUBLANES)
    q8 = qt_ref[pl.ds(r, SUBLANES), :].astype(_BF16)
    k8 = kh_ref[pl.ds(r, SUBLANES), :]
    v8 = v_ref[pl.ds(r, SUBLANES), :].astype(_BF16)
    d8 = dec_ref[pl.ds(r, SUBLANES), :]
    o8 = jnp.zeros((SUBLANES, LANES), _F32)
    for i in range(per_group):
        b = local_group * per_group + i
        mine = (row8 >= i * T) & (row8 < (i + 1) * T)
        S0 = sin_ref[b]
        o8 = jnp.where(mine, jnp.dot(q8, S0.astype(_BF16), preferred_element_type=_F32), o8)
        upd = pl.dot(jnp.where(mine, k8, 0.0).astype(_BF16), v8, trans_a=True)
        decay = jnp.broadcast_to(d8[i * T:i * T + 1], (LANES, LANES)).T
        sout_ref[b] = decay * S0 + upd
    oi_ref[pl.ds(r, SUBLANES), :] = o8


def _sample_finish(proj_ref, nw, os_ref, oi_ref, ybs_ref, oy_ref):
    o = os_ref[...] + oi_ref[...]
    o = o * _rms_scale(o) * nw * _silu(_group(proj_ref, 3))
    oy_ref[:, _O_COLS] = o.astype(oy_ref.dtype)
    oy_ref[:, _YB_COLS] = ybs_ref[...]


def _first_norm_kernel(xp_ref, xs_ref, xm_ref, w_ref, hn_ref, hnm_ref, *, n_prompt_tiles):
    i = pl.program_id(0)
    x = jnp.where(i < n_prompt_tiles, xp_ref[...], xs_ref[...])
    hn_ref[...] = (x * _rms_scale(x) * w_ref[...]).astype(_BF16)

    @pl.when(i == 0)
    def _():
        xm = xm_ref[...]
        hnm_ref[...] = (xm * _rms_scale(xm) * w_ref[...]).astype(_BF16)


def _first_norm(xp, xs, xm, pre_w):
    Rp, D = xp.shape
    n_p = Rp // ROW_TILE
    n_meta = xm.shape[0]
    return pl.pallas_call(
        functools.partial(_first_norm_kernel, n_prompt_tiles=n_p), grid=(n_p + 1,),
        in_specs=[pl.BlockSpec((ROW_TILE, D), lambda i: (jnp.minimum(i, n_p - 1), 0)),
                  pl.BlockSpec((ROW_TILE, D), lambda i: (0, 0)),
                  pl.BlockSpec((n_meta, D), lambda i: (0, 0)),
                  pl.BlockSpec((None, 1, D), lambda i: (0, 0, 0))],
        out_specs=[pl.BlockSpec((ROW_TILE, D), lambda i: (i, 0)),
                   pl.BlockSpec((n_meta, D), lambda i: (0, 0))],
        out_shape=[jax.ShapeDtypeStruct((Rp + ROW_TILE, D), _BF16),
                   jax.ShapeDtypeStruct((n_meta, D), _BF16)],
        compiler_params=pltpu.CompilerParams(dimension_semantics=("arbitrary",),
                                             vmem_limit_bytes=VMEM_LIMIT),
        name="first_norm",
    )(xp, xs, xm, pre_w)


def _mixer_kernel(hnm_ref, hn_ref, hns_ref, *refs, layer, depth, n_prompt_tiles, tiles_per_seq, seq_len,
                  aliased):
    slabs = refs[:N_GROUPS]
    woa_ref, wob_ref = refs[N_GROUPS:N_GROUPS + 2]
    refs = refs[N_GROUPS + 2:]
    if aliased:
        refs = refs[1:]
    (prm_ref, sin_ref, cpad_ref,
     oy_ref, oys_ref, oym_ref, sp_ref, cp_ref, ss_ref, us_ref, wo16_ref,
     w_scr, proj_ref, next_ref, projm_ref, projs_ref, st_ref, cb_ref, st0_ref, cb0_ref, g_ref,
     qt_ref, kh_ref, dec_ref, v_ref, os_ref, oi_ref, ybs_ref) = refs
    s = pl.program_id(1)
    n_meta = hnm_ref.shape[0]
    groups_per_step = sin_ref.shape[0] * seq_len // SUBLANES
    lb = _lower_bound(prm_ref[0:depth], layer)
    cw = prm_ref[depth:depth + CONV_W]
    nw = prm_ref[depth + CONV_W:depth + CONV_W + 1]

    @pl.when(s == 0)
    def _():
        for g, slab in enumerate(slabs):
            w_scr[:, g * LANES:(g + 1) * LANES] = slab[...].astype(_BF16)
        wo16_ref[_O_COLS.start:_O_COLS.stop] = woa_ref[...].astype(_BF16)
        wo16_ref[_YB_COLS.start:_YB_COLS.stop] = wob_ref[...].astype(_BF16)
        projm_ref[...] = jnp.dot(hnm_ref[...], w_scr[...], preferred_element_type=_F32)
        st_ref[...] = jnp.zeros_like(st_ref)
        cb_ref[...] = jnp.zeros_like(cb_ref)
        _chunked_head_mixer(projm_ref, lb, nw, cw, st_ref, cb_ref, g_ref, n_meta, oym_ref)
        st0_ref[...] = st_ref[...]
        cb0_ref[...] = cb_ref[...]
        projs_ref[...] = jnp.dot(hns_ref[...], w_scr[...], preferred_element_type=_F32)
        _sample_prepare(projs_ref, lb, cw, cpad_ref, us_ref, qt_ref, kh_ref, dec_ref, v_ref, os_ref, ybs_ref,
                        seq_len)
        proj_ref[...] = jnp.dot(hn_ref[...], w_scr[...], preferred_element_type=_F32)

    def prompt_step(project_next):
        @pl.when((s - 1) % tiles_per_seq == 0)
        def _():
            st_ref[...] = st0_ref[...]
            cb_ref[...] = cb0_ref[...]

        def filler(c, n):
            if project_next:
                n_cols = N_GROUPS * LANES // NEXT_COL_PIECES
                n_rows = MIX_TILE * NEXT_COL_PIECES // n
                rows = slice((c // NEXT_COL_PIECES) * n_rows, (c // NEXT_COL_PIECES + 1) * n_rows)
                cols = slice((c % NEXT_COL_PIECES) * n_cols, (c % NEXT_COL_PIECES + 1) * n_cols)
                next_ref[rows, cols] = jnp.dot(hn_ref[rows, :], w_scr[:, cols], preferred_element_type=_F32)
            for i in range(c * groups_per_step // n, (c + 1) * groups_per_step // n):
                _sample_group((s - 1) * groups_per_step + i, i, sin_ref, ss_ref, qt_ref, kh_ref, dec_ref,
                              v_ref, oi_ref, seq_len)

        _chunked_head_mixer(proj_ref, lb, nw, cw, st_ref, cb_ref, g_ref, PROMPT_CHUNK, oy_ref, filler=filler)
        cp_ref[...] = cb_ref[...]
        sp_ref[...] = st_ref[...].T
        if project_next:
            proj_ref[...] = next_ref[...]

    @pl.when((s >= 1) & (s < n_prompt_tiles))
    def _():
        prompt_step(True)

    @pl.when(s == n_prompt_tiles)
    def _():
        prompt_step(False)
        _sample_finish(projs_ref, nw, os_ref, oi_ref, ybs_ref, oys_ref)


def _mixer(hnm, hn, w_in, w_out, head_params, state, cpad, prev_states, *, layer, tiles_per_seq, seq_len):
    R, D = hn.shape
    n_meta = hnm.shape[0]
    depth = w_in.shape[0]
    W = w_in.shape[2] // N_GROUPS
    H = W // LANES
    n_p = (R - ROW_TILE) // MIX_TILE
    n_seq = state.shape[1]
    seq_per_step = n_seq // n_p
    assert seq_per_step * n_p == n_seq and (seq_per_step * seq_len) % (SUBLANES * (MIX_TILE // PROMPT_CHUNK)) == 0
    n_steps = n_p + 1
    aliased = prev_states is not None

    def weights_head(j, s):
        return jnp.minimum(j + (s >= 2).astype(jnp.int32), H - 1)

    def next_tile_of(s):
        return jnp.clip(s, 0, n_p - 1)

    def prompt_tile_of(s):
        return jnp.clip(s - 1, 0, n_p - 1)

    kern = functools.partial(_mixer_kernel, layer=layer, depth=depth, n_prompt_tiles=n_p,
                             tiles_per_seq=tiles_per_seq, seq_len=seq_len, aliased=aliased)
    seq_state_spec = pl.BlockSpec((None, seq_per_step, None, LANES, LANES),
                                  lambda j, s: (layer, prompt_tile_of(s), j, 0, 0))
    in_specs = (
        [pl.BlockSpec((n_meta, D), lambda j, s: (0, 0)),
         pl.BlockSpec((MIX_TILE, D), lambda j, s: (next_tile_of(s), 0)),
         pl.BlockSpec((ROW_TILE, D), lambda j, s: (R // ROW_TILE - 1, 0))]
        + [pl.BlockSpec((None, D, LANES), lambda j, s, g=g: (layer, 0, g * H + weights_head(j, s)))
           for g in range(N_GROUPS)]
        + [pl.BlockSpec((None, LANES, D), lambda j, s: (layer, weights_head(j, s), 0)),
           pl.BlockSpec((None, LANES, D), lambda j, s: (layer, H + weights_head(j, s), 0))]
        + ([pl.BlockSpec(memory_space=pl.ANY)] if aliased else [])
        + [pl.BlockSpec((None, head_params.shape[1], LANES), lambda j, s: (layer, 0, j)),
           seq_state_spec,
           pl.BlockSpec((None, ROW_TILE, LANES), lambda j, s: (layer, 0, j))])
    out_specs = [
        pl.BlockSpec((MIX_TILE, 2 * LANES), lambda j, s: (prompt_tile_of(s), j)),
        pl.BlockSpec((ROW_TILE, 2 * LANES), lambda j, s: (0, j)),
        pl.BlockSpec((n_meta, 2 * LANES), lambda j, s: (0, j)),
        pl.BlockSpec((None, None, LANES, LANES), lambda j, s: (prompt_tile_of(s), j, 0, 0)),
        pl.BlockSpec((None, SUBLANES, LANES), lambda j, s: (prompt_tile_of(s), 0, j)),
        seq_state_spec,
        pl.BlockSpec((ROW_TILE, LANES), lambda j, s: (0, j)),
        pl.BlockSpec((2 * LANES, D), lambda j, s: (j, 0)),
    ]
    out_shape = [
        jax.ShapeDtypeStruct((R - ROW_TILE, 2 * W), _BF16),
        jax.ShapeDtypeStruct((ROW_TILE, 2 * W), _BF16),
        jax.ShapeDtypeStruct((n_meta, 2 * W), _BF16),
        jax.ShapeDtypeStruct((n_p, H, LANES, LANES), _F32),
        jax.ShapeDtypeStruct((n_p, SUBLANES, W), _F32),
        jax.ShapeDtypeStruct(state.shape, _F32),
        jax.ShapeDtypeStruct((ROW_TILE, W), _F32),
        jax.ShapeDtypeStruct((2 * W, D), _BF16),
    ]
    scratch = [
        pltpu.VMEM((D, N_GROUPS * LANES), _BF16),
        pltpu.VMEM((MIX_TILE, N_GROUPS * LANES), _F32),
        pltpu.VMEM((MIX_TILE, N_GROUPS * LANES), _F32),
        pltpu.VMEM((n_meta, N_GROUPS * LANES), _F32),
        pltpu.VMEM((ROW_TILE, N_GROUPS * LANES), _F32),
        pltpu.VMEM((LANES, LANES), _F32),
        pltpu.VMEM((SUBLANES, LANES), _F32),
        pltpu.VMEM((LANES, LANES), _F32),
        pltpu.VMEM((SUBLANES, LANES), _F32),
        pltpu.VMEM((PROMPT_CHUNK, LANES), _F32),
    ] + [pltpu.VMEM((ROW_TILE, LANES), _F32)] * 6 + [
        pltpu.VMEM((ROW_TILE, LANES), _BF16)]
    args = ([hnm, hn, hn] + [w_in] * N_GROUPS + [w_out, w_out] + ([prev_states] if aliased else [])
            + [head_params, state, cpad])
    return pl.pallas_call(
        kern, grid=(H, n_steps), in_specs=in_specs, out_specs=out_specs, out_shape=out_shape,
        scratch_shapes=scratch,
        input_output_aliases={N_GROUPS + 5: 5} if aliased else {},
        compiler_params=pltpu.CompilerParams(dimension_semantics=("arbitrary", "arbitrary"),
                                             vmem_limit_bytes=VMEM_LIMIT),
        name="mixer",
    )(*args)


def _out_kernel(*refs, n_prompt_tiles, first, last):
    refs = list(refs)
    if first:
        hp_ref, hs_ref = refs[:2]
        refs = refs[2:]
    else:
        h_ref = refs.pop(0)
    hm_ref, oy_ref, oys_ref, oym_ref, wo_ref, post_ref = refs[:6]
    refs = refs[6:]
    if last:
        yp_ref, ys_ref = refs
    else:
        pre_ref, hout_ref, hn_ref, hmout_ref, hnm_ref = refs
    i = pl.program_id(0)

    def mixed(h, oy):
        mix = jnp.dot(oy, wo_ref[...], preferred_element_type=_F32)
        return h + mix * _rms_scale(mix) * post_ref[...]

    if not last:
        @pl.when(i == 0)
        def _():
            hm = mixed(hm_ref[...], oym_ref[...])
            hmout_ref[...] = hm
            hnm_ref[...] = (hm * _rms_scale(hm) * pre_ref[...]).astype(_BF16)

    if first:
        h_in = jnp.where(i < n_prompt_tiles, hp_ref[...], hs_ref[...])
    else:
        h_in = h_ref[...]
    h = mixed(h_in, jnp.where(i < n_prompt_tiles, oy_ref[...], oys_ref[...]))
    if last:
        @pl.when(i < n_prompt_tiles)
        def _():
            yp_ref[...] = h

        @pl.when(i >= n_prompt_tiles)
        def _():
            ys_ref[...] = h
    else:
        hout_ref[...] = h
        hn_ref[...] = (h * _rms_scale(h) * pre_ref[...]).astype(_BF16)


def _out(h_args, hm, oy, oys, oym, wo16, post_w, pre_w, *, layer, first, last):
    Rp, W2 = oy.shape
    R = Rp + oys.shape[0]
    D = wo16.shape[1]
    n_meta = oym.shape[0]
    n_p = Rp // ROW_TILE
    kern = functools.partial(_out_kernel, n_prompt_tiles=n_p, first=first, last=last)
    tile_spec = pl.BlockSpec((ROW_TILE, D), lambda i: (i, 0))
    prompt_spec = pl.BlockSpec((ROW_TILE, D), lambda i: (jnp.minimum(i, n_p - 1), 0))
    sample_spec = pl.BlockSpec((ROW_TILE, D), lambda i: (0, 0))
    meta_spec = pl.BlockSpec((n_meta, D), lambda i: (0, 0))
    in_specs = ([prompt_spec, sample_spec] if first else [tile_spec]) + [
        meta_spec,
        pl.BlockSpec((ROW_TILE, W2), lambda i: (jnp.minimum(i, n_p - 1), 0)),
        pl.BlockSpec((ROW_TILE, W2), lambda i: (0, 0)),
        pl.BlockSpec((n_meta, W2), lambda i: (0, 0)),
        pl.BlockSpec((W2, D), lambda i: (0, 0), pipeline_mode=pl.Buffered(1)),
        pl.BlockSpec((None, 1, D), lambda i: (layer, 0, 0)),
    ]
    args = list(h_args) + [hm, oy, oys, oym, wo16, post_w]
    if last:
        out_specs = [prompt_spec, sample_spec]
        out_shape = [jax.ShapeDtypeStruct((n_p * ROW_TILE, D), _F32),
                     jax.ShapeDtypeStruct((ROW_TILE, D), _F32)]
    else:
        in_specs.append(pl.BlockSpec((None, 1, D), lambda i: (layer + 1, 0, 0)))
        args.append(pre_w)
        out_specs = [tile_spec, tile_spec, meta_spec, meta_spec]
        out_shape = [jax.ShapeDtypeStruct((R, D), _F32), jax.ShapeDtypeStruct((R, D), _BF16),
                     jax.ShapeDtypeStruct((n_meta, D), _F32), jax.ShapeDtypeStruct((n_meta, D), _BF16)]
    return pl.pallas_call(
        kern, grid=(n_p + 1,), in_specs=in_specs, out_specs=out_specs, out_shape=out_shape,
        compiler_params=pltpu.CompilerParams(dimension_semantics=("arbitrary",),
                                             vmem_limit_bytes=VMEM_LIMIT),
        name="out_proj",
    )(*args)


def kernel(x_prompt, x_sample, state_hgrn, state_conv, meta_tokens, w_in, conv_w, lb_param,
           hgrn_norm_w, w_out, pre_norm_w, post_norm_w):
    B, T, D = x_prompt.shape
    DB, DT, _ = x_sample.shape
    depth = w_in.shape[0]
    n_meta = meta_tokens.shape[0]
    W = D // 2
    H = W // LANES
    assert W % LANES == 0 and SUBLANES % DT == 0 and n_meta % (2 * SUBLANES) == 0
    assert T % MIX_TILE == 0 and DB * DT == ROW_TILE and n_meta <= PROMPT_CHUNK
    assert w_in.shape == (depth, D, N_GROUPS * W) and state_conv.shape[2] == CONV_W - 1

    pre = pre_norm_w.reshape(depth, 1, D)
    post = post_norm_w.reshape(depth, 1, D)
    n_rows = depth + CONV_W + 1
    head_params = jnp.concatenate(
        [jnp.broadcast_to(lb_param[None], (depth, depth, W)), conv_w,
         jnp.tile(hgrn_norm_w, (1, H))[:, None, :],
         jnp.zeros((depth, -n_rows % SUBLANES, W), _F32)], axis=1)
    xp = x_prompt.reshape(B * T, D)
    xs = x_sample.reshape(DB * DT, D)
    cpad = jnp.concatenate(
        [state_conv, jnp.zeros((depth, DB, DT - (CONV_W - 1), W), _F32)], axis=2
    ).reshape(depth, DB * DT, W)
    tiles_per_seq = T // MIX_TILE
    last_tiles = slice(tiles_per_seq - 1, None, tiles_per_seq)

    hn, hnm = _first_norm(xp, xs, meta_tokens, pre)
    h_args, hm = (xp, xs), meta_tokens
    s_sample = None
    s_list, c_list, u_list = [], [], []
    for layer in range(depth):
        oy, oys, oym, s_p, c_p, s_sample, u_s, wo16 = _mixer(
            hnm, hn, w_in, w_out, head_params, state_hgrn, cpad, s_sample,
            layer=layer, tiles_per_seq=tiles_per_seq, seq_len=DT)
        s_list.append(s_p[last_tiles])
        c_list.append(c_p[last_tiles, SUBLANES - (CONV_W - 1):])
        u_list.append(u_s.reshape(DB, DT, W)[:, DT - (CONV_W - 1):])
        last = layer + 1 == depth
        outs = _out(h_args, hm, oy, oys, oym, wo16, post, pre, layer=layer, first=layer == 0, last=last)
        if not last:
            h_all, hn, hm, hnm = outs
            h_args = (h_all,)
    y_prompt = outs[0].reshape(B, T, D)
    y_sample = outs[1].reshape(DB, DT, D)
    return (y_prompt, y_sample, jnp.stack(s_list), jnp.stack(c_list), s_sample, jnp.stack(u_list))
```

```python
import functools

import jax
import jax.numpy as jnp
from jax import lax
from jax.experimental import pallas as pl
from jax.experimental.pallas import tpu as pltpu

LANES = 128
SUBLANES = 8
N_GROUPS = 8
CONV_W = 3
EPS = 1e-6
F_FLOOR = 1e-30
LOG2_E = 1.4426950408889634
ROW_TILE = 512
MIX_TILE = 1024
NEXT_COL_PIECES = 4
PROMPT_CHUNK = 128
VMEM_LIMIT = 56 * 1024 * 1024

_F32 = jnp.float32
_BF16 = jnp.bfloat16
_NT = (((1,), (1,)), ((), ()))
_O_COLS = slice(0, LANES)
_YB_COLS = slice(LANES, 2 * LANES)


def _sigmoid(x):
    return 1.0 / (1.0 + jnp.exp(-x))


def _silu(x):
    return x * _sigmoid(x)


def _rms_scale(x):
    return lax.rsqrt(jnp.mean(x * x, axis=-1, keepdims=True) + EPS)


def _lower_bound(lbp, layer):
    e = jnp.exp(lbp - jnp.max(lbp, axis=0, keepdims=True))
    w = e / jnp.sum(e, axis=0, keepdims=True)
    r = lax.broadcasted_iota(jnp.int32, w.shape, 0)
    return jnp.sum(jnp.where((r >= 1) & (r <= layer), w, 0.0), axis=0, keepdims=True)


def _forget_gates(fz, lb):
    one_m = 1.0 - lb
    f = lb + one_m * _sigmoid(fz)
    return jnp.log(jnp.maximum(f, F_FLOOR)), one_m * _sigmoid(-fz)


def _shift_rows(x, s):
    return pltpu.roll(x, s % x.shape[0], 0)


def _cumsum_rows(x, row):
    s = 1
    while s < x.shape[0]:
        x = x + jnp.where(row >= s, _shift_rows(x, s), 0.0)
        s *= 2
    return x


def _anchor_rows(g_ref, G, n, row):
    C = G.shape[0]
    half = n // 2
    if n == 2:
        return jnp.where((row & 1) == 1, _shift_rows(G, 1), G)
    if n == 4:
        pos = row & 3
        return jnp.where(pos == 0, _shift_rows(G, -1),
                         jnp.where(pos == 1, G,
                                   jnp.where(pos == 2, _shift_rows(G, 1), _shift_rows(G, 2))))
    parts = []
    for a in range(C // n):
        r = g_ref[pl.ds(a * n + half - 1, 1), :]
        parts.append(jnp.broadcast_to(r, (n, LANES)))
    return parts[0] if len(parts) == 1 else jnp.concatenate(parts, axis=0)


def _intra_chunk_scores(q, kk, G2, g_ref, row, col):
    C = q.shape[0]
    A = jnp.where(row == col, jnp.sum(q * kk, axis=-1, keepdims=True), 0.0)
    q16 = q.astype(_BF16)
    k16 = kk.astype(_BF16)
    n = 2
    while n <= C:
        half = n // 2
        E = jnp.exp2(-jnp.abs(G2 - _anchor_rows(g_ref, G2, n, row))).astype(_BF16)
        P = lax.dot_general(q16 * E, k16 * E, _NT, preferred_element_type=_F32)
        shift = n.bit_length() - 1
        mask = ((row >> shift) == (col >> shift)) & ((row & (n - 1)) >= half) & ((col & (n - 1)) < half)
        A = jnp.where(mask, P, A)
        n *= 2
    return A


def _group(ref, g, rows=slice(None)):
    return ref[rows, g * LANES:(g + 1) * LANES]


def _conv_gate(bg, zb, u, um1, um2, cw):
    c = cw[0:1] * um2 + cw[1:2] * um1 + cw[2:3] * u
    return bg * c * _silu(zb)


def _chunked_head_mixer(proj_ref, lb, nw, cw, st_ref, cb_ref, g_ref, chunk, oy_ref, filler=None):
    TM = proj_ref.shape[0]
    C = chunk
    n_chunks = TM // C
    row = lax.broadcasted_iota(jnp.int32, (C, 1), 0)
    col = lax.broadcasted_iota(jnp.int32, (1, C), 1)

    def scores(c):
        rows = slice(c * C, (c + 1) * C)
        q = _group(proj_ref, 0, rows)
        v16 = _group(proj_ref, 2, rows).astype(_BF16)
        g, kk = _forget_gates(_group(proj_ref, 1, rows), lb)
        G2 = _cumsum_rows(g * LOG2_E, row)
        g_ref[0:C] = G2
        A = _intra_chunk_scores(q, kk, G2, g_ref, row, col).astype(_BF16)
        GL2 = G2[C - 1:C]
        return (A, v16, (q * jnp.exp2(G2)).astype(_BF16), (kk * jnp.exp2(GL2 - G2)).astype(_BF16),
                jnp.exp2(GL2))

    def advance(c, staged):
        A, v16, qdec, khat, decay = staged
        rows = slice(c * C, (c + 1) * C)
        St = st_ref[...]
        o = jnp.dot(jnp.concatenate([A, qdec], axis=1),
                    jnp.concatenate([v16, St.T.astype(_BF16)], axis=0), preferred_element_type=_F32)
        st_ref[...] = St * decay + pl.dot(v16, khat, trans_a=True)
        o = o * _rms_scale(o) * nw * _silu(_group(proj_ref, 3, rows))
        oy_ref[rows, _O_COLS] = o.astype(oy_ref.dtype)

    for c in range(n_chunks):
        if filler is not None:
            filler(c, n_chunks)
        advance(c, scores(c))

    u = _group(proj_ref, 5) * _group(proj_ref, 6)
    prev = cb_ref[...]
    trow = lax.broadcasted_iota(jnp.int32, (TM, 1), 0)
    um1 = jnp.where(trow >= 1, _shift_rows(u, 1), prev[7:8])
    um2 = jnp.where(trow >= 2, _shift_rows(u, 2), jnp.where(trow == 0, prev[6:7], prev[7:8]))
    oy_ref[:, _YB_COLS] = _conv_gate(_group(proj_ref, 4), _group(proj_ref, 7), u, um1, um2,
                                     cw).astype(oy_ref.dtype)
    cb_ref[...] = u[TM - SUBLANES:TM]


def _sample_prepare(proj_ref, lb, cw, cpad_ref, u_ref, qt_ref, kh_ref, dec_ref, v_ref, os_ref, ybs_ref, seq_len):
    T = seq_len
    TM = proj_ref.shape[0]
    row = lax.broadcasted_iota(jnp.int32, (TM, 1), 0)
    pos = row % T
    q = _group(proj_ref, 0)
    v = _group(proj_ref, 2)
    g, kk = _forget_gates(_group(proj_ref, 1), lb)

    G = g
    for d in range(1, T):
        G = G + jnp.where(pos >= d, _shift_rows(g, d), 0.0)
    o = jnp.sum(q * kk, axis=-1, keepdims=True) * v
    for d in range(1, T):
        decay = jnp.exp(jnp.minimum(G - _shift_rows(G, d), 0.0))
        a = jnp.sum(jnp.where(pos >= d, q * _shift_rows(kk, d) * decay, 0.0), axis=-1, keepdims=True)
        o = o + a * _shift_rows(v, d)
    GL = G
    for d in range(1, T):
        GL = jnp.where(pos == T - 1 - d, _shift_rows(G, -d), GL)
    qt_ref[...] = q * jnp.exp(G)
    kh_ref[...] = kk * jnp.exp(GL - G)
    dec_ref[...] = jnp.exp(GL)
    v_ref[...] = v
    os_ref[...] = o

    u = _group(proj_ref, 5) * _group(proj_ref, 6)
    e = cpad_ref[...]
    um1 = jnp.where(pos >= 1, _shift_rows(u, 1), _shift_rows(e, -1))
    um2 = jnp.where(pos >= 2, _shift_rows(u, 2), e)
    ybs_ref[...] = _conv_gate(_group(proj_ref, 4), _group(proj_ref, 7), u, um1, um2, cw).astype(ybs_ref.dtype)
    u_ref[...] = u


def _sample_group(group, local_group, sin_ref, sout_ref, qt_ref, kh_ref, dec_ref, v_ref, oi_ref, seq_len):
    T = seq_len
    per_group = SUBLANES // T
    row8 = lax.broadcasted_iota(jnp.int32, (SUBLANES, 1), 0)
    r = pl.multiple_of(group * SUBLANES, SUBLANES)
    q8 = qt_ref[pl.ds(r, SUBLANES), :].astype(_BF16)
    k8 = kh_ref[pl.ds(r, SUBLANES), :]
    v8 = v_ref[pl.ds(r, SUBLANES), :].astype(_BF16)
    d8 = dec_ref[pl.ds(r, SUBLANES), :]
    o8 = jnp.zeros((SUBLANES, LANES), _F32)
    for i in range(per_group):
        b = local_group * per_group + i
        mine = (row8 >= i * T) & (row8 < (i + 1) * T)
        S0 = sin_ref[b]
        o8 = jnp.where(mine, jnp.dot(q8, S0.astype(_BF16), preferred_element_type=_F32), o8)
        upd = pl.dot(jnp.where(mine, k8, 0.0).astype(_BF16), v8, trans_a=True)
        decay = jnp.broadcast_to(d8[i * T:i * T + 1], (LANES, LANES)).T
        sout_ref[b] = decay * S0 + upd
    oi_ref[pl.ds(r, SUBLANES), :] = o8


def _sample_finish(proj_ref, nw, os_ref, oi_ref, ybs_ref, oy_ref):
    o = os_ref[...] + oi_ref[...]
    o = o * _rms_scale(o) * nw * _silu(_group(proj_ref, 3))
    oy_ref[:, _O_COLS] = o.astype(oy_ref.dtype)
    oy_ref[:, _YB_COLS] = ybs_ref[...]


def _first_norm_kernel(xp_ref, xs_ref, xm_ref, w_ref, hn_ref, hnm_ref, *, n_prompt_tiles):
    i = pl.program_id(0)
    x = jnp.where(i < n_prompt_tiles, xp_ref[...], xs_ref[...])
    hn_ref[...] = (x * _rms_scale(x) * w_ref[...]).astype(_BF16)

    @pl.when(i == 0)
    def _():
        xm = xm_ref[...]
        hnm_ref[...] = (xm * _rms_scale(xm) * w_ref[...]).astype(_BF16)


def _first_norm(xp, xs, xm, pre_w):
    Rp, D = xp.shape
    n_p = Rp // ROW_TILE
    n_meta = xm.shape[0]
    return pl.pallas_call(
        functools.partial(_first_norm_kernel, n_prompt_tiles=n_p), grid=(n_p + 1,),
        in_specs=[pl.BlockSpec((ROW_TILE, D), lambda i: (jnp.minimum(i, n_p - 1), 0)),
                  pl.BlockSpec((ROW_TILE, D), lambda i: (0, 0)),
                  pl.BlockSpec((n_meta, D), lambda i: (0, 0)),
                  pl.BlockSpec((None, 1, D), lambda i: (0, 0, 0))],
        out_specs=[pl.BlockSpec((ROW_TILE, D), lambda i: (i, 0)),
                   pl.BlockSpec((n_meta, D), lambda i: (0, 0))],
        out_shape=[jax.ShapeDtypeStruct((Rp + ROW_TILE, D), _BF16),
                   jax.ShapeDtypeStruct((n_meta, D), _BF16)],
        compiler_params=pltpu.CompilerParams(dimension_semantics=("arbitrary",),
                                             vmem_limit_bytes=VMEM_LIMIT),
        name="first_norm",
    )(xp, xs, xm, pre_w)


def _mixer_kernel(hnm_ref, hn_ref, hns_ref, *refs, layer, depth, n_prompt_tiles, tiles_per_seq, seq_len,
                  aliased):
    slabs = refs[:N_GROUPS]
    woa_ref, wob_ref = refs[N_GROUPS:N_GROUPS + 2]
    refs = refs[N_GROUPS + 2:]
    if aliased:
        refs = refs[1:]
    (prm_ref, sin_ref, cpad_ref,
     oy_ref, oys_ref, oym_ref, sp_ref, cp_ref, ss_ref, us_ref, wo16_ref,
     w_scr, proj_ref, next_ref, projm_ref, projs_ref, st_ref, cb_ref, st0_ref, cb0_ref, g_ref,
     qt_ref, kh_ref, dec_ref, v_ref, os_ref, oi_ref, ybs_ref) = refs
    s = pl.program_id(1)
    n_meta = hnm_ref.shape[0]
    groups_per_step = sin_ref.shape[0] * seq_len // SUBLANES
    lb = _lower_bound(prm_ref[0:depth], layer)
    cw = prm_ref[depth:depth + CONV_W]
    nw = prm_ref[depth + CONV_W:depth + CONV_W + 1]

    @pl.when(s == 0)
    def _():
        for g, slab in enumerate(slabs):
            w_scr[:, g * LANES:(g + 1) * LANES] = slab[...].astype(_BF16)
        wo16_ref[_O_COLS.start:_O_COLS.stop] = woa_ref[...].astype(_BF16)
        wo16_ref[_YB_COLS.start:_YB_COLS.stop] = wob_ref[...].astype(_BF16)
        projm_ref[...] = jnp.dot(hnm_ref[...], w_scr[...], preferred_element_type=_F32)
        st_ref[...] = jnp.zeros_like(st_ref)
        cb_ref[...] = jnp.zeros_like(cb_ref)
        _chunked_head_mixer(projm_ref, lb, nw, cw, st_ref, cb_ref, g_ref, n_meta, oym_ref)
        st0_ref[...] = st_ref[...]
        cb0_ref[...] = cb_ref[...]
        projs_ref[...] = jnp.dot(hns_ref[...], w_scr[...], preferred_element_type=_F32)
        _sample_prepare(projs_ref, lb, cw, cpad_ref, us_ref, qt_ref, kh_ref, dec_ref, v_ref, os_ref, ybs_ref,
                        seq_len)
        proj_ref[...] = jnp.dot(hn_ref[...], w_scr[...], preferred_element_type=_F32)

    def prompt_step(project_next):
        @pl.when((s - 1) % tiles_per_seq == 0)
        def _():
            st_ref[...] = st0_ref[...]
            cb_ref[...] = cb0_ref[...]

        def filler(c, n):
            if project_next:
                n_cols = N_GROUPS * LANES // NEXT_COL_PIECES
                n_rows = MIX_TILE * NEXT_COL_PIECES // n
                rows = slice((c // NEXT_COL_PIECES) * n_rows, (c // NEXT_COL_PIECES + 1) * n_rows)
                cols = slice((c % NEXT_COL_PIECES) * n_cols, (c % NEXT_COL_PIECES + 1) * n_cols)
                next_ref[rows, cols] = jnp.dot(hn_ref[rows, :], w_scr[:, cols], preferred_element_type=_F32)
            for i in range(c * groups_per_step // n, (c + 1) * groups_per_step // n):
                _sample_group((s - 1) * groups_per_step + i, i, sin_ref, ss_ref, qt_ref, kh_ref, dec_ref,
                              v_ref, oi_ref, seq_len)

        _chunked_head_mixer(proj_ref, lb, nw, cw, st_ref, cb_ref, g_ref, PROMPT_CHUNK, oy_ref, filler=filler)
        cp_ref[...] = cb_ref[...]
        sp_ref[...] = st_ref[...].T
        if project_next:
            proj_ref[...] = next_ref[...]

    @pl.when((s >= 1) & (s < n_prompt_tiles))
    def _():
        prompt_step(True)

    @pl.when(s == n_prompt_tiles)
    def _():
        prompt_step(False)
        _sample_finish(projs_ref, nw, os_ref, oi_ref, ybs_ref, oys_ref)


def _mixer(hnm, hn, w_in, w_out, head_params, state, cpad, prev_states, *, layer, tiles_per_seq, seq_len):
    R, D = hn.shape
    n_meta = hnm.shape[0]
    depth = w_in.shape[0]
    W = w_in.shape[2] // N_GROUPS
    H = W // LANES
    n_p = (R - ROW_TILE) // MIX_TILE
    n_seq = state.shape[1]
    seq_per_step = n_seq // n_p
    assert seq_per_step * n_p == n_seq and (seq_per_step * seq_len) % (SUBLANES * (MIX_TILE // PROMPT_CHUNK)) == 0
    n_steps = n_p + 1
    aliased = prev_states is not None

    def weights_head(j, s):
        return jnp.minimum(j + (s >= 2).astype(jnp.int32), H - 1)

    def next_tile_of(s):
        return jnp.clip(s, 0, n_p - 1)

    def prompt_tile_of(s):
        return jnp.clip(s - 1, 0, n_p - 1)

    kern = functools.partial(_mixer_kernel, layer=layer, depth=depth, n_prompt_tiles=n_p,
                             tiles_per_seq=tiles_per_seq, seq_len=seq_len, aliased=aliased)
    seq_state_spec = pl.BlockSpec((None, seq_per_step, None, LANES, LANES),
                                  lambda j, s: (layer, prompt_tile_of(s), j, 0, 0))
    in_specs = (
        [pl.BlockSpec((n_meta, D), lambda j, s: (0, 0)),
         pl.BlockSpec((MIX_TILE, D), lambda j, s: (next_tile_of(s), 0)),
         pl.BlockSpec((ROW_TILE, D), lambda j, s: (R // ROW_TILE - 1, 0))]
        + [pl.BlockSpec((None, D, LANES), lambda j, s, g=g: (layer, 0, g * H + weights_head(j, s)))
           for g in range(N_GROUPS)]
        + [pl.BlockSpec((None, LANES, D), lambda j, s: (layer, weights_head(j, s), 0)),
           pl.BlockSpec((None, LANES, D), lambda j, s: (layer, H + weights_head(j, s), 0))]
        + ([pl.BlockSpec(memory_space=pl.ANY)] if aliased else [])
        + [pl.BlockSpec((None, head_params.shape[1], LANES), lambda j, s: (layer, 0, j)),
           seq_state_spec,
           pl.BlockSpec((None, ROW_TILE, LANES), lambda j, s: (layer, 0, j))])
    out_specs = [
        pl.BlockSpec((MIX_TILE, 2 * LANES), lambda j, s: (prompt_tile_of(s), j)),
        pl.BlockSpec((ROW_TILE, 2 * LANES), lambda j, s: (0, j)),
        pl.BlockSpec((n_meta, 2 * LANES), lambda j, s: (0, j)),
        pl.BlockSpec((None, None, LANES, LANES), lambda j, s: (prompt_tile_of(s), j, 0, 0)),
        pl.BlockSpec((None, SUBLANES, LANES), lambda j, s: (prompt_tile_of(s), 0, j)),
        seq_state_spec,
        pl.BlockSpec((ROW_TILE, LANES), lambda j, s: (0, j)),
        pl.BlockSpec((2 * LANES, D), lambda j, s: (j, 0)),
    ]
    out_shape = [
        jax.ShapeDtypeStruct((R - ROW_TILE, 2 * W), _BF16),
        jax.ShapeDtypeStruct((ROW_TILE, 2 * W), _BF16),
        jax.ShapeDtypeStruct((n_meta, 2 * W), _BF16),
        jax.ShapeDtypeStruct((n_p, H, LANES, LANES), _F32),
        jax.ShapeDtypeStruct((n_p, SUBLANES, W), _F32),
        jax.ShapeDtypeStruct(state.shape, _F32),
        jax.ShapeDtypeStruct((ROW_TILE, W), _F32),
        jax.ShapeDtypeStruct((2 * W, D), _BF16),
    ]
    scratch = [
        pltpu.VMEM((D, N_GROUPS * LANES), _BF16),
        pltpu.VMEM((MIX_TILE, N_GROUPS * LANES), _F32),
        pltpu.VMEM((MIX_TILE, N_GROUPS * LANES), _F32),
        pltpu.VMEM((n_meta, N_GROUPS * LANES), _F32),
        pltpu.VMEM((ROW_TILE, N_GROUPS * LANES), _F32),
        pltpu.VMEM((LANES, LANES), _F32),
        pltpu.VMEM((SUBLANES, LANES), _F32),
        pltpu.VMEM((LANES, LANES), _F32),
        pltpu.VMEM((SUBLANES, LANES), _F32),
        pltpu.VMEM((PROMPT_CHUNK, LANES), _F32),
    ] + [pltpu.VMEM((ROW_TILE, LANES), _F32)] * 6 + [
        pltpu.VMEM((ROW_TILE, LANES), _BF16)]
    args = ([hnm, hn, hn] + [w_in] * N_GROUPS + [w_out, w_out] + ([prev_states] if aliased else [])
            + [head_params, state, cpad])
    return pl.pallas_call(
        kern, grid=(H, n_steps), in_specs=in_specs, out_specs=out_specs, out_shape=out_shape,
        scratch_shapes=scratch,
        input_output_aliases={N_GROUPS + 5: 5} if aliased else {},
        compiler_params=pltpu.CompilerParams(dimension_semantics=("arbitrary", "arbitrary"),
                                             vmem_limit_bytes=VMEM_LIMIT),
        name="mixer",
    )(*args)


def _out_kernel(*refs, n_prompt_tiles, first, last):
    refs = list(refs)
    if first:
        hp_ref, hs_ref = refs[:2]
        refs = refs[2:]
    else:
        h_ref = refs.pop(0)
    hm_ref, oy_ref, oys_ref, oym_ref, wo_ref, post_ref = refs[:6]
    refs = refs[6:]
    if last:
        yp_ref, ys_ref = refs
    else:
        pre_ref, hout_ref, hn_ref, hmout_ref, hnm_ref = refs
    i = pl.program_id(0)

    def mixed(h, oy):
        mix = jnp.dot(oy, wo_ref[...], preferred_element_type=_F32)
        return h + mix * _rms_scale(mix) * post_ref[...]

    if not last:
        @pl.when(i == 0)
        def _():
            hm = mixed(hm_ref[...], oym_ref[...])
            hmout_ref[...] = hm
            hnm_ref[...] = (hm * _rms_scale(hm) * pre_ref[...]).astype(_BF16)

    if first:
        h_in = jnp.where(i < n_prompt_tiles, hp_ref[...], hs_ref[...])
    else:
        h_in = h_ref[...]
    h = mixed(h_in, jnp.where(i < n_prompt_tiles, oy_ref[...], oys_ref[...]))
    if last:
        @pl.when(i < n_prompt_tiles)
        def _():
            yp_ref[...] = h

        @pl.when(i >= n_prompt_tiles)
        def _():
            ys_ref[...] = h
    else:
        hout_ref[...] = h
        hn_ref[...] = (h * _rms_scale(h) * pre_ref[...]).astype(_BF16)


def _out(h_args, hm, oy, oys, oym, wo16, post_w, pre_w, *, layer, first, last):
    Rp, W2 = oy.shape
    R = Rp + oys.shape[0]
    D = wo16.shape[1]
    n_meta = oym.shape[0]
    n_p = Rp // ROW_TILE
    kern = functools.partial(_out_kernel, n_prompt_tiles=n_p, first=first, last=last)
    tile_spec = pl.BlockSpec((ROW_TILE, D), lambda i: (i, 0))
    prompt_spec = pl.BlockSpec((ROW_TILE, D), lambda i: (jnp.minimum(i, n_p - 1), 0))
    sample_spec = pl.BlockSpec((ROW_TILE, D), lambda i: (0, 0))
    meta_spec = pl.BlockSpec((n_meta, D), lambda i: (0, 0))
    in_specs = ([prompt_spec, sample_spec] if first else [tile_spec]) + [
        meta_spec,
        pl.BlockSpec((ROW_TILE, W2), lambda i: (jnp.minimum(i, n_p - 1), 0)),
        pl.BlockSpec((ROW_TILE, W2), lambda i: (0, 0)),
        pl.BlockSpec((n_meta, W2), lambda i: (0, 0)),
        pl.BlockSpec((W2, D), lambda i: (0, 0), pipeline_mode=pl.Buffered(1)),
        pl.BlockSpec((None, 1, D), lambda i: (layer, 0, 0)),
    ]
    args = list(h_args) + [hm, oy, oys, oym, wo16, post_w]
    if last:
        out_specs = [prompt_spec, sample_spec]
        out_shape = [jax.ShapeDtypeStruct((n_p * ROW_TILE, D), _F32),
                     jax.ShapeDtypeStruct((ROW_TILE, D), _F32)]
    else:
        in_specs.append(pl.BlockSpec((None, 1, D), lambda i: (layer + 1, 0, 0)))
        args.append(pre_w)
        out_specs = [tile_spec, tile_spec, meta_spec, meta_spec]
        out_shape = [jax.ShapeDtypeStruct((R, D), _F32), jax.ShapeDtypeStruct((R, D), _BF16),
                     jax.ShapeDtypeStruct((n_meta, D), _F32), jax.ShapeDtypeStruct((n_meta, D), _BF16)]
    return pl.pallas_call(
        kern, grid=(n_p + 1,), in_specs=in_specs, out_specs=out_specs, out_shape=out_shape,
        compiler_params=pltpu.CompilerParams(dimension_semantics=("arbitrary",),
                                             vmem_limit_bytes=VMEM_LIMIT),
        name="out_proj",
    )(*args)


def kernel(x_prompt, x_sample, state_hgrn, state_conv, meta_tokens, w_in, conv_w, lb_param,
           hgrn_norm_w, w_out, pre_norm_w, post_norm_w):
    B, T, D = x_prompt.shape
    DB, DT, _ = x_sample.shape
    depth = w_in.shape[0]
    n_meta = meta_tokens.shape[0]
    W = D // 2
    H = W // LANES
    assert W % LANES == 0 and SUBLANES % DT == 0 and n_meta % (2 * SUBLANES) == 0
    assert T % MIX_TILE == 0 and DB * DT == ROW_TILE and n_meta <= PROMPT_CHUNK
    assert w_in.shape == (depth, D, N_GROUPS * W) and state_conv.shape[2] == CONV_W - 1

    pre = pre_norm_w.reshape(depth, 1, D)
    post = post_norm_w.reshape(depth, 1, D)
    n_rows = depth + CONV_W + 1
    head_params = jnp.concatenate(
        [jnp.broadcast_to(lb_param[None], (depth, depth, W)), conv_w,
         jnp.tile(hgrn_norm_w, (1, H))[:, None, :],
         jnp.zeros((depth, -n_rows % SUBLANES, W), _F32)], axis=1)
    xp = x_prompt.reshape(B * T, D)
    xs = x_sample.reshape(DB * DT, D)
    cpad = jnp.concatenate(
        [state_conv, jnp.zeros((depth, DB, DT - (CONV_W - 1), W), _F32)], axis=2
    ).reshape(depth, DB * DT, W)
    tiles_per_seq = T // MIX_TILE
    last_tiles = slice(tiles_per_seq - 1, None, tiles_per_seq)

    hn, hnm = _first_norm(xp, xs, meta_tokens, pre)
    h_args, hm = (xp, xs), meta_tokens
    s_sample = None
    s_list, c_list, u_list = [], [], []
    for layer in range(depth):
        oy, oys, oym, s_p, c_p, s_sample, u_s, wo16 = _mixer(
            hnm, hn, w_in, w_out, head_params, state_hgrn, cpad, s_sample,
            layer=layer, tiles_per_seq=tiles_per_seq, seq_len=DT)
        s_list.append(s_p[last_tiles])
        c_list.append(c_p[last_tiles, SUBLANES - (CONV_W - 1):])
        u_list.append(u_s.reshape(DB, DT, W)[:, DT - (CONV_W - 1):])
        last = layer + 1 == depth
        outs = _out(h_args, hm, oy, oys, oym, wo16, post, pre, layer=layer, first=layer == 0, last=last)
        if not last:
            h_all, hn, hm, hnm = outs
            h_args = (h_all,)
    y_prompt = outs[0].reshape(B, T, D)
    y_sample = outs[1].reshape(DB, DT, D)
    return (y_prompt, y_sample, jnp.stack(s_list), jnp.stack(c_list), s_sample, jnp.stack(u_list))
```

```python
import functools

import jax
import jax.numpy as jnp
from jax import lax
from jax.experimental import pallas as pl
from jax.experimental.pallas import tpu as pltpu

LANES = 128
SUBLANES = 8
N_GROUPS = 8
CONV_W = 3
EPS = 1e-6
F_FLOOR = 1e-30
LOG2_E = 1.4426950408889634
ROW_TILE = 512
MIX_TILE = 1024
NEXT_COL_PIECES = 4
PROMPT_CHUNK = 128
VMEM_LIMIT = 56 * 1024 * 1024

_F32 = jnp.float32
_BF16 = jnp.bfloat16
_NT = (((1,), (1,)), ((), ()))
_O_COLS = slice(0, LANES)
_YB_COLS = slice(LANES, 2 * LANES)


def _sigmoid(x):
    return 1.0 / (1.0 + jnp.exp(-x))


def _silu(x):
    return x * _sigmoid(x)


def _rms_scale(x):
    return lax.rsqrt(jnp.mean(x * x, axis=-1, keepdims=True) + EPS)


def _lower_bound(lbp, layer):
    e = jnp.exp(lbp - jnp.max(lbp, axis=0, keepdims=True))
    w = e / jnp.sum(e, axis=0, keepdims=True)
    r = lax.broadcasted_iota(jnp.int32, w.shape, 0)
    return jnp.sum(jnp.where((r >= 1) & (r <= layer), w, 0.0), axis=0, keepdims=True)


def _forget_gates(fz, lb):
    one_m = 1.0 - lb
    f = lb + one_m * _sigmoid(fz)
    return jnp.log(jnp.maximum(f, F_FLOOR)), one_m * _sigmoid(-fz)


def _shift_rows(x, s):
    return pltpu.roll(x, s % x.shape[0], 0)


def _cumsum_rows(x, row):
    s = 1
    while s < x.shape[0]:
        x = x + jnp.where(row >= s, _shift_rows(x, s), 0.0)
        s *= 2
    return x


def _anchor_rows(g_ref, G, n, row):
    C = G.shape[0]
    half = n // 2
    if n == 2:
        return jnp.where((row & 1) == 1, _shift_rows(G, 1), G)
    if n == 4:
        pos = row & 3
        return jnp.where(pos == 0, _shift_rows(G, -1),
                         jnp.where(pos == 1, G,
                                   jnp.where(pos == 2, _shift_rows(G, 1), _shift_rows(G, 2))))
    parts = []
    for a in range(C // n):
        r = g_ref[pl.ds(a * n + half - 1, 1), :]
        parts.append(jnp.broadcast_to(r, (n, LANES)))
    return parts[0] if len(parts) == 1 else jnp.concatenate(parts, axis=0)


def _intra_chunk_scores(q, kk, G2, g_ref, row, col):
    C = q.shape[0]
    A = jnp.where(row == col, jnp.sum(q * kk, axis=-1, keepdims=True), 0.0)
    q16 = q.astype(_BF16)
    k16 = kk.astype(_BF16)
    n = 2
    while n <= C:
        half = n // 2
        E = jnp.exp2(-jnp.abs(G2 - _anchor_rows(g_ref, G2, n, row))).astype(_BF16)
        P = lax.dot_general(q16 * E, k16 * E, _NT, preferred_element_type=_F32)
        shift = n.bit_length() - 1
        mask = ((row >> shift) == (col >> shift)) & ((row & (n - 1)) >= half) & ((col & (n - 1)) < half)
        A = jnp.where(mask, P, A)
        n *= 2
    return A


def _group(ref, g, rows=slice(None)):
    return ref[rows, g * LANES:(g + 1) * LANES]


def _conv_gate(bg, zb, u, um1, um2, cw):
    c = cw[0:1] * um2 + cw[1:2] * um1 + cw[2:3] * u
    return bg * c * _silu(zb)


def _chunked_head_mixer(proj_ref, lb, nw, cw, st_ref, cb_ref, g_ref, chunk, oy_ref, filler=None):
    TM = proj_ref.shape[0]
    C = chunk
    n_chunks = TM // C
    row = lax.broadcasted_iota(jnp.int32, (C, 1), 0)
    col = lax.broadcasted_iota(jnp.int32, (1, C), 1)

    def scores(c):
        rows = slice(c * C, (c + 1) * C)
        q = _group(proj_ref, 0, rows)
        v16 = _group(proj_ref, 2, rows).astype(_BF16)
        g, kk = _forget_gates(_group(proj_ref, 1, rows), lb)
        G2 = _cumsum_rows(g * LOG2_E, row)
        g_ref[0:C] = G2
        A = _intra_chunk_scores(q, kk, G2, g_ref, row, col).astype(_BF16)
        GL2 = G2[C - 1:C]
        return (A, v16, (q * jnp.exp2(G2)).astype(_BF16), (kk * jnp.exp2(GL2 - G2)).astype(_BF16),
                jnp.exp2(GL2))

    def advance(c, staged):
        A, v16, qdec, khat, decay = staged
        rows = slice(c * C, (c + 1) * C)
        St = st_ref[...]
        o = jnp.dot(jnp.concatenate([A, qdec], axis=1),
                    jnp.concatenate([v16, St.T.astype(_BF16)], axis=0), preferred_element_type=_F32)
        st_ref[...] = St * decay + pl.dot(v16, khat, trans_a=True)
        o = o * _rms_scale(o) * nw * _silu(_group(proj_ref, 3, rows))
        oy_ref[rows, _O_COLS] = o.astype(oy_ref.dtype)

    for c in range(n_chunks):
        if filler is not None:
            filler(c, n_chunks)
        advance(c, scores(c))

    u = _group(proj_ref, 5) * _group(proj_ref, 6)
    prev = cb_ref[...]
    trow = lax.broadcasted_iota(jnp.int32, (TM, 1), 0)
    um1 = jnp.where(trow >= 1, _shift_rows(u, 1), prev[7:8])
    um2 = jnp.where(trow >= 2, _shift_rows(u, 2), jnp.where(trow == 0, prev[6:7], prev[7:8]))
    oy_ref[:, _YB_COLS] = _conv_gate(_group(proj_ref, 4), _group(proj_ref, 7), u, um1, um2,
                                     cw).astype(oy_ref.dtype)
    cb_ref[...] = u[TM - SUBLANES:TM]


def _sample_prepare(proj_ref, lb, cw, cpad_ref, u_ref, qt_ref, kh_ref, dec_ref, v_ref, os_ref, ybs_ref, seq_len):
    T = seq_len
    TM = proj_ref.shape[0]
    row = lax.broadcasted_iota(jnp.int32, (TM, 1), 0)
    pos = row % T
    q = _group(proj_ref, 0)
    v = _group(proj_ref, 2)
    g, kk = _forget_gates(_group(proj_ref, 1), lb)

    G = g
    for d in range(1, T):
        G = G + jnp.where(pos >= d, _shift_rows(g, d), 0.0)
    o = jnp.sum(q * kk, axis=-1, keepdims=True) * v
    for d in range(1, T):
        decay = jnp.exp(jnp.minimum(G - _shift_rows(G, d), 0.0))
        a = jnp.sum(jnp.where(pos >= d, q * _shift_rows(kk, d) * decay, 0.0), axis=-1, keepdims=True)
        o = o + a * _shift_rows(v, d)
    GL = G
    for d in range(1, T):
        GL = jnp.where(pos == T - 1 - d, _shift_rows(G, -d), GL)
    qt_ref[...] = q * jnp.exp(G)
    kh_ref[...] = kk * jnp.exp(GL - G)
    dec_ref[...] = jnp.exp(GL)
    v_ref[...] = v
    os_ref[...] = o

    u = _group(proj_ref, 5) * _group(proj_ref, 6)
    e = cpad_ref[...]
    um1 = jnp.where(pos >= 1, _shift_rows(u, 1), _shift_rows(e, -1))
    um2 = jnp.where(pos >= 2, _shift_rows(u, 2), e)
    ybs_ref[...] = _conv_gate(_group(proj_ref, 4), _group(proj_ref, 7), u, um1, um2, cw).astype(ybs_ref.dtype)
    u_ref[...] = u


def _sample_group(group, local_group, sin_ref, sout_ref, qt_ref, kh_ref, dec_ref, v_ref, oi_ref, seq_len):
    T = seq_len
    per_group = SUBLANES // T
    row8 = lax.broadcasted_iota(jnp.int32, (SUBLANES, 1), 0)
    r = pl.multiple_of(group * SUBLANES, SUBLANES)
    q8 = qt_ref[pl.ds(r, SUBLANES), :].astype(_BF16)
    k8 = kh_ref[pl.ds(r, SUBLANES), :]
    v8 = v_ref[pl.ds(r, SUBLANES), :].astype(_BF16)
    d8 = dec_ref[pl.ds(r, SUBLANES), :]
    mine = [(row8 >= i * T) & (row8 < (i + 1) * T) for i in range(per_group)]
    states = [sin_ref[local_group * per_group + i] for i in range(per_group)]
    o_all = jnp.dot(q8, jnp.concatenate(states, axis=1).astype(_BF16), preferred_element_type=_F32)
    v_all = jnp.concatenate([jnp.where(m, v8, jnp.zeros_like(v8)) for m in mine], axis=1)
    upd_all = pl.dot(k8.astype(_BF16), v_all, trans_a=True)
    o8 = jnp.zeros((SUBLANES, LANES), _F32)
    for i in range(per_group):
        cols = slice(i * LANES, (i + 1) * LANES)
        o8 = jnp.where(mine[i], o_all[:, cols], o8)
        decay = jnp.broadcast_to(d8[i * T:i * T + 1], (LANES, LANES)).T
        sout_ref[local_group * per_group + i] = decay * states[i] + upd_all[:, cols]
    oi_ref[pl.ds(r, SUBLANES), :] = o8


def _sample_finish(proj_ref, nw, os_ref, oi_ref, ybs_ref, oy_ref):
    o = os_ref[...] + oi_ref[...]
    o = o * _rms_scale(o) * nw * _silu(_group(proj_ref, 3))
    oy_ref[:, _O_COLS] = o.astype(oy_ref.dtype)
    oy_ref[:, _YB_COLS] = ybs_ref[...]


def _first_norm_kernel(xp_ref, xs_ref, xm_ref, w_ref, hn_ref, hnm_ref, *, n_prompt_tiles):
    i = pl.program_id(0)
    x = jnp.where(i < n_prompt_tiles, xp_ref[...], xs_ref[...])
    hn_ref[...] = (x * _rms_scale(x) * w_ref[...]).astype(_BF16)

    @pl.when(i == 0)
    def _():
        xm = xm_ref[...]
        hnm_ref[...] = (xm * _rms_scale(xm) * w_ref[...]).astype(_BF16)


def _first_norm(xp, xs, xm, pre_w):
    Rp, D = xp.shape
    n_p = Rp // ROW_TILE
    n_meta = xm.shape[0]
    return pl.pallas_call(
        functools.partial(_first_norm_kernel, n_prompt_tiles=n_p), grid=(n_p + 1,),
        in_specs=[pl.BlockSpec((ROW_TILE, D), lambda i: (jnp.minimum(i, n_p - 1), 0)),
                  pl.BlockSpec((ROW_TILE, D), lambda i: (0, 0)),
                  pl.BlockSpec((n_meta, D), lambda i: (0, 0)),
                  pl.BlockSpec((None, 1, D), lambda i: (0, 0, 0))],
        out_specs=[pl.BlockSpec((ROW_TILE, D), lambda i: (i, 0)),
                   pl.BlockSpec((n_meta, D), lambda i: (0, 0))],
        out_shape=[jax.ShapeDtypeStruct((Rp + ROW_TILE, D), _BF16),
                   jax.ShapeDtypeStruct((n_meta, D), _BF16)],
        compiler_params=pltpu.CompilerParams(dimension_semantics=("arbitrary",),
                                             vmem_limit_bytes=VMEM_LIMIT),
        name="first_norm",
    )(xp, xs, xm, pre_w)


def _mixer_kernel(hnm_ref, hn_ref, hns_ref, *refs, layer, depth, n_prompt_tiles, tiles_per_seq, seq_len,
                  aliased):
    slabs = refs[:N_GROUPS]
    woa_ref, wob_ref = refs[N_GROUPS:N_GROUPS + 2]
    refs = refs[N_GROUPS + 2:]
    if aliased:
        refs = refs[1:]
    (prm_ref, sin_ref, cpad_ref,
     oy_ref, oys_ref, oym_ref, sp_ref, cp_ref, ss_ref, us_ref, wo16_ref,
     w_scr, proj_ref, next_ref, projm_ref, projs_ref, st_ref, cb_ref, st0_ref, cb0_ref, g_ref,
     qt_ref, kh_ref, dec_ref, v_ref, os_ref, oi_ref, ybs_ref) = refs
    s = pl.program_id(1)
    n_meta = hnm_ref.shape[0]
    groups_per_step = sin_ref.shape[0] * seq_len // SUBLANES
    lb = _lower_bound(prm_ref[0:depth], layer)
    cw = prm_ref[depth:depth + CONV_W]
    nw = prm_ref[depth + CONV_W:depth + CONV_W + 1]

    @pl.when(s == 0)
    def _():
        for g, slab in enumerate(slabs):
            w_scr[:, g * LANES:(g + 1) * LANES] = slab[...].astype(_BF16)
        wo16_ref[_O_COLS.start:_O_COLS.stop] = woa_ref[...].astype(_BF16)
        wo16_ref[_YB_COLS.start:_YB_COLS.stop] = wob_ref[...].astype(_BF16)
        projm_ref[...] = jnp.dot(hnm_ref[...], w_scr[...], preferred_element_type=_F32)
        st_ref[...] = jnp.zeros_like(st_ref)
        cb_ref[...] = jnp.zeros_like(cb_ref)
        _chunked_head_mixer(projm_ref, lb, nw, cw, st_ref, cb_ref, g_ref, n_meta, oym_ref)
        st0_ref[...] = st_ref[...]
        cb0_ref[...] = cb_ref[...]
        projs_ref[...] = jnp.dot(hns_ref[...], w_scr[...], preferred_element_type=_F32)
        _sample_prepare(projs_ref, lb, cw, cpad_ref, us_ref, qt_ref, kh_ref, dec_ref, v_ref, os_ref, ybs_ref,
                        seq_len)
        proj_ref[...] = jnp.dot(hn_ref[...], w_scr[...], preferred_element_type=_F32)

    def prompt_step(project_next):
        @pl.when((s - 1) % tiles_per_seq == 0)
        def _():
            st_ref[...] = st0_ref[...]
            cb_ref[...] = cb0_ref[...]

        def filler(c, n):
            if project_next:
                n_cols = N_GROUPS * LANES // NEXT_COL_PIECES
                n_rows = MIX_TILE * NEXT_COL_PIECES // n
                rows = slice((c // NEXT_COL_PIECES) * n_rows, (c // NEXT_COL_PIECES + 1) * n_rows)
                cols = slice((c % NEXT_COL_PIECES) * n_cols, (c % NEXT_COL_PIECES + 1) * n_cols)
                next_ref[rows, cols] = jnp.dot(hn_ref[rows, :], w_scr[:, cols], preferred_element_type=_F32)
            for i in range(c * groups_per_step // n, (c + 1) * groups_per_step // n):
                _sample_group((s - 1) * groups_per_step + i, i, sin_ref, ss_ref, qt_ref, kh_ref, dec_ref,
                              v_ref, oi_ref, seq_len)

        _chunked_head_mixer(proj_ref, lb, nw, cw, st_ref, cb_ref, g_ref, PROMPT_CHUNK, oy_ref, filler=filler)
        cp_ref[...] = cb_ref[...]
        sp_ref[...] = st_ref[...].T
        if project_next:
            proj_ref[...] = next_ref[...]

    @pl.when((s >= 1) & (s < n_prompt_tiles))
    def _():
        prompt_step(True)

    @pl.when(s == n_prompt_tiles)
    def _():
        prompt_step(False)
        _sample_finish(projs_ref, nw, os_ref, oi_ref, ybs_ref, oys_ref)


def _mixer(hnm, hn, w_in, w_out, head_params, state, cpad, prev_states, *, layer, tiles_per_seq, seq_len):
    R, D = hn.shape
    n_meta = hnm.shape[0]
    depth = w_in.shape[0]
    W = w_in.shape[2] // N_GROUPS
    H = W // LANES
    n_p = (R - ROW_TILE) // MIX_TILE
    n_seq = state.shape[1]
    seq_per_step = n_seq // n_p
    assert seq_per_step * n_p == n_seq and (seq_per_step * seq_len) % (SUBLANES * (MIX_TILE // PROMPT_CHUNK)) == 0
    n_steps = n_p + 1
    aliased = prev_states is not None

    def weights_head(j, s):
        return jnp.minimum(j + (s >= 2).astype(jnp.int32), H - 1)

    def next_tile_of(s):
        return jnp.clip(s, 0, n_p - 1)

    def prompt_tile_of(s):
        return jnp.clip(s - 1, 0, n_p - 1)

    kern = functools.partial(_mixer_kernel, layer=layer, depth=depth, n_prompt_tiles=n_p,
                             tiles_per_seq=tiles_per_seq, seq_len=seq_len, aliased=aliased)
    seq_state_spec = pl.BlockSpec((None, seq_per_step, None, LANES, LANES),
                                  lambda j, s: (layer, prompt_tile_of(s), j, 0, 0))
    in_specs = (
        [pl.BlockSpec((n_meta, D), lambda j, s: (0, 0)),
         pl.BlockSpec((MIX_TILE, D), lambda j, s: (next_tile_of(s), 0)),
         pl.BlockSpec((ROW_TILE, D), lambda j, s: (R // ROW_TILE - 1, 0))]
        + [pl.BlockSpec((None, D, LANES), lambda j, s, g=g: (layer, 0, g * H + weights_head(j, s)))
           for g in range(N_GROUPS)]
        + [pl.BlockSpec((None, LANES, D), lambda j, s: (layer, weights_head(j, s), 0)),
           pl.BlockSpec((None, LANES, D), lambda j, s: (layer, H + weights_head(j, s), 0))]
        + ([pl.BlockSpec(memory_space=pl.ANY)] if aliased else [])
        + [pl.BlockSpec((None, head_params.shape[1], LANES), lambda j, s: (layer, 0, j)),
           seq_state_spec,
           pl.BlockSpec((None, ROW_TILE, LANES), lambda j, s: (layer, 0, j))])
    out_specs = [
        pl.BlockSpec((MIX_TILE, 2 * LANES), lambda j, s: (prompt_tile_of(s), j)),
        pl.BlockSpec((ROW_TILE, 2 * LANES), lambda j, s: (0, j)),
        pl.BlockSpec((n_meta, 2 * LANES), lambda j, s: (0, j)),
        pl.BlockSpec((None, None, LANES, LANES), lambda j, s: (prompt_tile_of(s), j, 0, 0)),
        pl.BlockSpec((None, SUBLANES, LANES), lambda j, s: (prompt_tile_of(s), 0, j)),
        seq_state_spec,
        pl.BlockSpec((ROW_TILE, LANES), lambda j, s: (0, j)),
        pl.BlockSpec((2 * LANES, D), lambda j, s: (j, 0)),
    ]
    out_shape = [
        jax.ShapeDtypeStruct((R - ROW_TILE, 2 * W), _BF16),
        jax.ShapeDtypeStruct((ROW_TILE, 2 * W), _BF16),
        jax.ShapeDtypeStruct((n_meta, 2 * W), _BF16),
        jax.ShapeDtypeStruct((n_p, H, LANES, LANES), _F32),
        jax.ShapeDtypeStruct((n_p, SUBLANES, W), _F32),
        jax.ShapeDtypeStruct(state.shape, _F32),
        jax.ShapeDtypeStruct((ROW_TILE, W), _F32),
        jax.ShapeDtypeStruct((2 * W, D), _BF16),
    ]
    scratch = [
        pltpu.VMEM((D, N_GROUPS * LANES), _BF16),
        pltpu.VMEM((MIX_TILE, N_GROUPS * LANES), _F32),
        pltpu.VMEM((MIX_TILE, N_GROUPS * LANES), _F32),
        pltpu.VMEM((n_meta, N_GROUPS * LANES), _F32),
        pltpu.VMEM((ROW_TILE, N_GROUPS * LANES), _F32),
        pltpu.VMEM((LANES, LANES), _F32),
        pltpu.VMEM((SUBLANES, LANES), _F32),
        pltpu.VMEM((LANES, LANES), _F32),
        pltpu.VMEM((SUBLANES, LANES), _F32),
        pltpu.VMEM((PROMPT_CHUNK, LANES), _F32),
    ] + [pltpu.VMEM((ROW_TILE, LANES), _F32)] * 6 + [
        pltpu.VMEM((ROW_TILE, LANES), _BF16)]
    args = ([hnm, hn, hn] + [w_in] * N_GROUPS + [w_out, w_out] + ([prev_states] if aliased else [])
            + [head_params, state, cpad])
    return pl.pallas_call(
        kern, grid=(H, n_steps), in_specs=in_specs, out_specs=out_specs, out_shape=out_shape,
        scratch_shapes=scratch,
        input_output_aliases={N_GROUPS + 5: 5} if aliased else {},
        compiler_params=pltpu.CompilerParams(dimension_semantics=("arbitrary", "arbitrary"),
                                             vmem_limit_bytes=VMEM_LIMIT),
        name="mixer",
    )(*args)


def _out_kernel(*refs, n_prompt_tiles, first, last):
    refs = list(refs)
    if first:
        hp_ref, hs_ref = refs[:2]
        refs = refs[2:]
    else:
        h_ref = refs.pop(0)
    hm_ref, oy_ref, oys_ref, oym_ref, wo_ref, post_ref = refs[:6]
    refs = refs[6:]
    if last:
        yp_ref, ys_ref = refs
    else:
        pre_ref, hout_ref, hn_ref, hmout_ref, hnm_ref = refs
    i = pl.program_id(0)

    def mixed(h, oy):
        mix = jnp.dot(oy, wo_ref[...], preferred_element_type=_F32)
        return h + mix * _rms_scale(mix) * post_ref[...]

    if not last:
        @pl.when(i == 0)
        def _():
            hm = mixed(hm_ref[...], oym_ref[...])
            hmout_ref[...] = hm
            hnm_ref[...] = (hm * _rms_scale(hm) * pre_ref[...]).astype(_BF16)

    if first:
        h_in = jnp.where(i < n_prompt_tiles, hp_ref[...], hs_ref[...])
    else:
        h_in = h_ref[...]
    h = mixed(h_in, jnp.where(i < n_prompt_tiles, oy_ref[...], oys_ref[...]))
    if last:
        @pl.when(i < n_prompt_tiles)
        def _():
            yp_ref[...] = h

        @pl.when(i >= n_prompt_tiles)
        def _():
            ys_ref[...] = h
    else:
        hout_ref[...] = h
        hn_ref[...] = (h * _rms_scale(h) * pre_ref[...]).astype(_BF16)


def _out(h_args, hm, oy, oys, oym, wo16, post_w, pre_w, *, layer, first, last):
    Rp, W2 = oy.shape
    R = Rp + oys.shape[0]
    D = wo16.shape[1]
    n_meta = oym.shape[0]
    n_p = Rp // ROW_TILE
    kern = functools.partial(_out_kernel, n_prompt_tiles=n_p, first=first, last=last)
    tile_spec = pl.BlockSpec((ROW_TILE, D), lambda i: (i, 0))
    prompt_spec = pl.BlockSpec((ROW_TILE, D), lambda i: (jnp.minimum(i, n_p - 1), 0))
    sample_spec = pl.BlockSpec((ROW_TILE, D), lambda i: (0, 0))
    meta_spec = pl.BlockSpec((n_meta, D), lambda i: (0, 0))
    in_specs = ([prompt_spec, sample_spec] if first else [tile_spec]) + [
        meta_spec,
        pl.BlockSpec((ROW_TILE, W2), lambda i: (jnp.minimum(i, n_p - 1), 0)),
        pl.BlockSpec((ROW_TILE, W2), lambda i: (0, 0)),
        pl.BlockSpec((n_meta, W2), lambda i: (0, 0)),
        pl.BlockSpec((W2, D), lambda i: (0, 0), pipeline_mode=pl.Buffered(1)),
        pl.BlockSpec((None, 1, D), lambda i: (layer, 0, 0)),
    ]
    args = list(h_args) + [hm, oy, oys, oym, wo16, post_w]
    if last:
        out_specs = [prompt_spec, sample_spec]
        out_shape = [jax.ShapeDtypeStruct((n_p * ROW_TILE, D), _F32),
                     jax.ShapeDtypeStruct((ROW_TILE, D), _F32)]
    else:
        in_specs.append(pl.BlockSpec((None, 1, D), lambda i: (layer + 1, 0, 0)))
        args.append(pre_w)
        out_specs = [tile_spec, tile_spec, meta_spec, meta_spec]
        out_shape = [jax.ShapeDtypeStruct((R, D), _F32), jax.ShapeDtypeStruct((R, D), _BF16),
                     jax.ShapeDtypeStruct((n_meta, D), _F32), jax.ShapeDtypeStruct((n_meta, D), _BF16)]
    return pl.pallas_call(
        kern, grid=(n_p + 1,), in_specs=in_specs, out_specs=out_specs, out_shape=out_shape,
        compiler_params=pltpu.CompilerParams(dimension_semantics=("arbitrary",),
                                             vmem_limit_bytes=VMEM_LIMIT),
        name="out_proj",
    )(*args)


def kernel(x_prompt, x_sample, state_hgrn, state_conv, meta_tokens, w_in, conv_w, lb_param,
           hgrn_norm_w, w_out, pre_norm_w, post_norm_w):
    B, T, D = x_prompt.shape
    DB, DT, _ = x_sample.shape
    depth = w_in.shape[0]
    n_meta = meta_tokens.shape[0]
    W = D // 2
    H = W // LANES
    assert W % LANES == 0 and SUBLANES % DT == 0 and n_meta % (2 * SUBLANES) == 0
    assert T % MIX_TILE == 0 and DB * DT == ROW_TILE and n_meta <= PROMPT_CHUNK
    assert w_in.shape == (depth, D, N_GROUPS * W) and state_conv.shape[2] == CONV_W - 1

    pre = pre_norm_w.reshape(depth, 1, D)
    post = post_norm_w.reshape(depth, 1, D)
    n_rows = depth + CONV_W + 1
    head_params = jnp.concatenate(
        [jnp.broadcast_to(lb_param[None], (depth, depth, W)), conv_w,
         jnp.tile(hgrn_norm_w, (1, H))[:, None, :],
         jnp.zeros((depth, -n_rows % SUBLANES, W), _F32)], axis=1)
    xp = x_prompt.reshape(B * T, D)
    xs = x_sample.reshape(DB * DT, D)
    cpad = jnp.concatenate(
        [state_conv, jnp.zeros((depth, DB, DT - (CONV_W - 1), W), _F32)], axis=2
    ).reshape(depth, DB * DT, W)
    tiles_per_seq = T // MIX_TILE
    last_tiles = slice(tiles_per_seq - 1, None, tiles_per_seq)

    hn, hnm = _first_norm(xp, xs, meta_tokens, pre)
    h_args, hm = (xp, xs), meta_tokens
    s_sample = None
    s_list, c_list, u_list = [], [], []
    for layer in range(depth):
        oy, oys, oym, s_p, c_p, s_sample, u_s, wo16 = _mixer(
            hnm, hn, w_in, w_out, head_params, state_hgrn, cpad, s_sample,
            layer=layer, tiles_per_seq=tiles_per_seq, seq_len=DT)
        s_list.append(s_p[last_tiles])
        c_list.append(c_p[last_tiles, SUBLANES - (CONV_W - 1):])
        u_list.append(u_s.reshape(DB, DT, W)[:, DT - (CONV_W - 1):])
        last = layer + 1 == depth
        outs = _out(h_args, hm, oy, oys, oym, wo16, post, pre, layer=layer, first=layer == 0, last=last)
        if not last:
            h_all, hn, hm, hnm = outs
            h_args = (h_all,)
    y_prompt = outs[0].reshape(B, T, D)
    y_sample = outs[1].reshape(DB, DT, D)
    return (y_prompt, y_sample, jnp.stack(s_list), jnp.stack(c_list), s_sample, jnp.stack(u_list))
```

```python
import functools

import jax
import jax.numpy as jnp
from jax import lax
from jax.experimental import pallas as pl
from jax.experimental.pallas import tpu as pltpu

LANES = 128
SUBLANES = 8
N_GROUPS = 8
CONV_W = 3
EPS = 1e-6
F_FLOOR = 1e-30
LOG2_E = 1.4426950408889634
ROW_TILE = 512
MIX_TILE = 1024
NEXT_COL_PIECES = 4
PROMPT_CHUNK = 128
OUT_EPILOGUE_ROWS = 16
VMEM_LIMIT = 56 * 1024 * 1024

_F32 = jnp.float32
_BF16 = jnp.bfloat16
_NT = (((1,), (1,)), ((), ()))
_O_COLS = slice(0, LANES)
_YB_COLS = slice(LANES, 2 * LANES)


def _sigmoid(x):
    return 1.0 / (1.0 + jnp.exp(-x))


def _silu(x):
    return x * _sigmoid(x)


def _rms_scale(x):
    return lax.rsqrt(jnp.mean(x * x, axis=-1, keepdims=True) + EPS)


def _lower_bound(lbp, layer):
    e = jnp.exp(lbp - jnp.max(lbp, axis=0, keepdims=True))
    w = e / jnp.sum(e, axis=0, keepdims=True)
    r = lax.broadcasted_iota(jnp.int32, w.shape, 0)
    return jnp.sum(jnp.where((r >= 1) & (r <= layer), w, 0.0), axis=0, keepdims=True)


def _forget_gates(fz, lb):
    one_m = 1.0 - lb
    f = lb + one_m * _sigmoid(fz)
    return jnp.log(jnp.maximum(f, F_FLOOR)), one_m * _sigmoid(-fz)


def _shift_rows(x, s):
    return pltpu.roll(x, s % x.shape[0], 0)


def _cumsum_rows(x, row):
    s = 1
    while s < x.shape[0]:
        x = x + jnp.where(row >= s, _shift_rows(x, s), 0.0)
        s *= 2
    return x


def _anchor_rows(g_ref, G, n, row):
    C = G.shape[0]
    half = n // 2
    if n == 2:
        return jnp.where((row & 1) == 1, _shift_rows(G, 1), G)
    if n == 4:
        pos = row & 3
        return jnp.where(pos == 0, _shift_rows(G, -1),
                         jnp.where(pos == 1, G,
                                   jnp.where(pos == 2, _shift_rows(G, 1), _shift_rows(G, 2))))
    parts = []
    for a in range(C // n):
        r = g_ref[pl.ds(a * n + half - 1, 1), :]
        parts.append(jnp.broadcast_to(r, (n, LANES)))
    return parts[0] if len(parts) == 1 else jnp.concatenate(parts, axis=0)


def _intra_chunk_scores(q, kk, G2, g_ref, row, col):
    C = q.shape[0]
    A = jnp.where(row == col, jnp.sum(q * kk, axis=-1, keepdims=True), 0.0)
    q16 = q.astype(_BF16)
    k16 = kk.astype(_BF16)
    n = 2
    while n <= C:
        half = n // 2
        E = jnp.exp2(-jnp.abs(G2 - _anchor_rows(g_ref, G2, n, row))).astype(_BF16)
        P = lax.dot_general(q16 * E, k16 * E, _NT, preferred_element_type=_F32)
        shift = n.bit_length() - 1
        mask = ((row >> shift) == (col >> shift)) & ((row & (n - 1)) >= half) & ((col & (n - 1)) < half)
        A = jnp.where(mask, P, A)
        n *= 2
    return A


def _group(ref, g, rows=slice(None)):
    return ref[rows, g * LANES:(g + 1) * LANES]


def _conv_gate(bg, zb, u, um1, um2, cw):
    c = cw[0:1] * um2 + cw[1:2] * um1 + cw[2:3] * u
    return bg * c * _silu(zb)


def _chunked_head_mixer(proj_ref, lb, nw, cw, st_ref, cb_ref, g_ref, chunk, oy_ref, filler=None):
    TM = proj_ref.shape[0]
    C = chunk
    n_chunks = TM // C
    row = lax.broadcasted_iota(jnp.int32, (C, 1), 0)
    col = lax.broadcasted_iota(jnp.int32, (1, C), 1)

    def scores(c):
        rows = slice(c * C, (c + 1) * C)
        q = _group(proj_ref, 0, rows)
        v16 = _group(proj_ref, 2, rows).astype(_BF16)
        g, kk = _forget_gates(_group(proj_ref, 1, rows), lb)
        G2 = _cumsum_rows(g * LOG2_E, row)
        g_ref[0:C] = G2
        A = _intra_chunk_scores(q, kk, G2, g_ref, row, col).astype(_BF16)
        GL2 = G2[C - 1:C]
        return (A, v16, (q * jnp.exp2(G2)).astype(_BF16), (kk * jnp.exp2(GL2 - G2)).astype(_BF16),
                jnp.exp2(GL2))

    def advance(c, staged):
        A, v16, qdec, khat, decay = staged
        rows = slice(c * C, (c + 1) * C)
        St = st_ref[...]
        o = jnp.dot(jnp.concatenate([A, qdec], axis=1),
                    jnp.concatenate([v16, St.T.astype(_BF16)], axis=0), preferred_element_type=_F32)
        st_ref[...] = St * decay + pl.dot(v16, khat, trans_a=True)
        o = o * _rms_scale(o) * nw * _silu(_group(proj_ref, 3, rows))
        oy_ref[rows, _O_COLS] = o.astype(oy_ref.dtype)

    for c in range(n_chunks):
        if filler is not None:
            filler(c, n_chunks)
        advance(c, scores(c))

    u = _group(proj_ref, 5) * _group(proj_ref, 6)
    prev = cb_ref[...]
    trow = lax.broadcasted_iota(jnp.int32, (TM, 1), 0)
    um1 = jnp.where(trow >= 1, _shift_rows(u, 1), prev[7:8])
    um2 = jnp.where(trow >= 2, _shift_rows(u, 2), jnp.where(trow == 0, prev[6:7], prev[7:8]))
    oy_ref[:, _YB_COLS] = _conv_gate(_group(proj_ref, 4), _group(proj_ref, 7), u, um1, um2,
                                     cw).astype(oy_ref.dtype)
    cb_ref[...] = u[TM - SUBLANES:TM]


def _sample_prepare(proj_ref, lb, cw, cpad_ref, u_ref, qt_ref, kh_ref, dec_ref, v_ref, os_ref, ybs_ref, seq_len):
    T = seq_len
    TM = proj_ref.shape[0]
    row = lax.broadcasted_iota(jnp.int32, (TM, 1), 0)
    pos = row % T
    q = _group(proj_ref, 0)
    v = _group(proj_ref, 2)
    g, kk = _forget_gates(_group(proj_ref, 1), lb)

    G = g
    for d in range(1, T):
        G = G + jnp.where(pos >= d, _shift_rows(g, d), 0.0)
    o = jnp.sum(q * kk, axis=-1, keepdims=True) * v
    for d in range(1, T):
        decay = jnp.exp(jnp.minimum(G - _shift_rows(G, d), 0.0))
        a = jnp.sum(jnp.where(pos >= d, q * _shift_rows(kk, d) * decay, 0.0), axis=-1, keepdims=True)
        o = o + a * _shift_rows(v, d)
    GL = G
    for d in range(1, T):
        GL = jnp.where(pos == T - 1 - d, _shift_rows(G, -d), GL)
    qt_ref[...] = q * jnp.exp(G)
    kh_ref[...] = kk * jnp.exp(GL - G)
    dec_ref[...] = jnp.exp(GL)
    v_ref[...] = v
    os_ref[...] = o

    u = _group(proj_ref, 5) * _group(proj_ref, 6)
    e = cpad_ref[...]
    um1 = jnp.where(pos >= 1, _shift_rows(u, 1), _shift_rows(e, -1))
    um2 = jnp.where(pos >= 2, _shift_rows(u, 2), e)
    ybs_ref[...] = _conv_gate(_group(proj_ref, 4), _group(proj_ref, 7), u, um1, um2, cw).astype(ybs_ref.dtype)
    u_ref[...] = u


def _sample_group(group, local_group, sin_ref, sout_ref, qt_ref, kh_ref, dec_ref, v_ref, oi_ref, seq_len):
    T = seq_len
    per_group = SUBLANES // T
    row8 = lax.broadcasted_iota(jnp.int32, (SUBLANES, 1), 0)
    r = pl.multiple_of(group * SUBLANES, SUBLANES)
    q8 = qt_ref[pl.ds(r, SUBLANES), :].astype(_BF16)
    k8 = kh_ref[pl.ds(r, SUBLANES), :]
    v8 = v_ref[pl.ds(r, SUBLANES), :].astype(_BF16)
    d8 = dec_ref[pl.ds(r, SUBLANES), :]
    mine = [(row8 >= i * T) & (row8 < (i + 1) * T) for i in range(per_group)]
    states = [sin_ref[local_group * per_group + i] for i in range(per_group)]
    o_all = jnp.dot(q8, jnp.concatenate(states, axis=1).astype(_BF16), preferred_element_type=_F32)
    v_all = jnp.concatenate([jnp.where(m, v8, jnp.zeros_like(v8)) for m in mine], axis=1)
    upd_all = pl.dot(k8.astype(_BF16), v_all, trans_a=True)
    o8 = jnp.zeros((SUBLANES, LANES), _F32)
    for i in range(per_group):
        cols = slice(i * LANES, (i + 1) * LANES)
        o8 = jnp.where(mine[i], o_all[:, cols], o8)
        decay = jnp.broadcast_to(d8[i * T:i * T + 1], (LANES, LANES)).T
        sout_ref[local_group * per_group + i] = decay * states[i] + upd_all[:, cols]
    oi_ref[pl.ds(r, SUBLANES), :] = o8


def _sample_finish(proj_ref, nw, os_ref, oi_ref, ybs_ref, oy_ref):
    o = os_ref[...] + oi_ref[...]
    o = o * _rms_scale(o) * nw * _silu(_group(proj_ref, 3))
    oy_ref[:, _O_COLS] = o.astype(oy_ref.dtype)
    oy_ref[:, _YB_COLS] = ybs_ref[...]


def _first_norm_kernel(xp_ref, xs_ref, xm_ref, w_ref, hn_ref, hnm_ref, *, n_prompt_tiles):
    i = pl.program_id(0)
    x = jnp.where(i < n_prompt_tiles, xp_ref[...], xs_ref[...])
    hn_ref[...] = (x * _rms_scale(x) * w_ref[...]).astype(_BF16)

    @pl.when(i == 0)
    def _():
        xm = xm_ref[...]
        hnm_ref[...] = (xm * _rms_scale(xm) * w_ref[...]).astype(_BF16)


def _first_norm(xp, xs, xm, pre_w):
    Rp, D = xp.shape
    n_p = Rp // ROW_TILE
    n_meta = xm.shape[0]
    return pl.pallas_call(
        functools.partial(_first_norm_kernel, n_prompt_tiles=n_p), grid=(n_p + 1,),
        in_specs=[pl.BlockSpec((ROW_TILE, D), lambda i: (jnp.minimum(i, n_p - 1), 0)),
                  pl.BlockSpec((ROW_TILE, D), lambda i: (0, 0)),
                  pl.BlockSpec((n_meta, D), lambda i: (0, 0)),
                  pl.BlockSpec((None, 1, D), lambda i: (0, 0, 0))],
        out_specs=[pl.BlockSpec((ROW_TILE, D), lambda i: (i, 0)),
                   pl.BlockSpec((n_meta, D), lambda i: (0, 0))],
        out_shape=[jax.ShapeDtypeStruct((Rp + ROW_TILE, D), _BF16),
                   jax.ShapeDtypeStruct((n_meta, D), _BF16)],
        compiler_params=pltpu.CompilerParams(dimension_semantics=("arbitrary",),
                                             vmem_limit_bytes=VMEM_LIMIT),
        name="first_norm",
    )(xp, xs, xm, pre_w)


def _mixer_kernel(hnm_ref, hn_ref, hns_ref, *refs, layer, depth, n_prompt_tiles, tiles_per_seq, seq_len,
                  aliased):
    slabs = refs[:N_GROUPS]
    woa_ref, wob_ref = refs[N_GROUPS:N_GROUPS + 2]
    refs = refs[N_GROUPS + 2:]
    if aliased:
        refs = refs[1:]
    (prm_ref, sin_ref, cpad_ref,
     oy_ref, oys_ref, oym_ref, sp_ref, cp_ref, ss_ref, us_ref, wo16_ref,
     w_scr, proj_ref, next_ref, projm_ref, projs_ref, st_ref, cb_ref, st0_ref, cb0_ref, g_ref,
     qt_ref, kh_ref, dec_ref, v_ref, os_ref, oi_ref, ybs_ref) = refs
    s = pl.program_id(1)
    n_meta = hnm_ref.shape[0]
    groups_per_step = sin_ref.shape[0] * seq_len // SUBLANES
    lb = _lower_bound(prm_ref[0:depth], layer)
    cw = prm_ref[depth:depth + CONV_W]
    nw = prm_ref[depth + CONV_W:depth + CONV_W + 1]

    @pl.when(s == 0)
    def _():
        for g, slab in enumerate(slabs):
            w_scr[:, g * LANES:(g + 1) * LANES] = slab[...].astype(_BF16)
        wo16_ref[_O_COLS.start:_O_COLS.stop] = woa_ref[...].astype(_BF16)
        wo16_ref[_YB_COLS.start:_YB_COLS.stop] = wob_ref[...].astype(_BF16)
        projm_ref[...] = jnp.dot(hnm_ref[...], w_scr[...], preferred_element_type=_F32)
        st_ref[...] = jnp.zeros_like(st_ref)
        cb_ref[...] = jnp.zeros_like(cb_ref)
        _chunked_head_mixer(projm_ref, lb, nw, cw, st_ref, cb_ref, g_ref, n_meta, oym_ref)
        st0_ref[...] = st_ref[...]
        cb0_ref[...] = cb_ref[...]
        projs_ref[...] = jnp.dot(hns_ref[...], w_scr[...], preferred_element_type=_F32)
        _sample_prepare(projs_ref, lb, cw, cpad_ref, us_ref, qt_ref, kh_ref, dec_ref, v_ref, os_ref, ybs_ref,
                        seq_len)
        proj_ref[...] = jnp.dot(hn_ref[...], w_scr[...], preferred_element_type=_F32)

    def prompt_step(project_next):
        @pl.when((s - 1) % tiles_per_seq == 0)
        def _():
            st_ref[...] = st0_ref[...]
            cb_ref[...] = cb0_ref[...]

        def filler(c, n):
            if project_next:
                n_cols = N_GROUPS * LANES // NEXT_COL_PIECES
                n_rows = MIX_TILE * NEXT_COL_PIECES // n
                rows = slice((c // NEXT_COL_PIECES) * n_rows, (c // NEXT_COL_PIECES + 1) * n_rows)
                cols = slice((c % NEXT_COL_PIECES) * n_cols, (c % NEXT_COL_PIECES + 1) * n_cols)
                next_ref[rows, cols] = jnp.dot(hn_ref[rows, :], w_scr[:, cols], preferred_element_type=_F32)
            for i in range(c * groups_per_step // n, (c + 1) * groups_per_step // n):
                _sample_group((s - 1) * groups_per_step + i, i, sin_ref, ss_ref, qt_ref, kh_ref, dec_ref,
                              v_ref, oi_ref, seq_len)

        _chunked_head_mixer(proj_ref, lb, nw, cw, st_ref, cb_ref, g_ref, PROMPT_CHUNK, oy_ref, filler=filler)
        cp_ref[...] = cb_ref[...]
        sp_ref[...] = st_ref[...].T
        if project_next:
            proj_ref[...] = next_ref[...]

    @pl.when((s >= 1) & (s < n_prompt_tiles))
    def _():
        prompt_step(True)

    @pl.when(s == n_prompt_tiles)
    def _():
        prompt_step(False)
        _sample_finish(projs_ref, nw, os_ref, oi_ref, ybs_ref, oys_ref)


def _mixer(hnm, hn, w_in, w_out, head_params, state, cpad, prev_states, *, layer, tiles_per_seq, seq_len):
    R, D = hn.shape
    n_meta = hnm.shape[0]
    depth = w_in.shape[0]
    W = w_in.shape[2] // N_GROUPS
    H = W // LANES
    n_p = (R - ROW_TILE) // MIX_TILE
    n_seq = state.shape[1]
    seq_per_step = n_seq // n_p
    assert seq_per_step * n_p == n_seq and (seq_per_step * seq_len) % (SUBLANES * (MIX_TILE // PROMPT_CHUNK)) == 0
    n_steps = n_p + 1
    aliased = prev_states is not None

    def weights_head(j, s):
        return jnp.minimum(j + (s >= 2).astype(jnp.int32), H - 1)

    def next_tile_of(s):
        return jnp.clip(s, 0, n_p - 1)

    def prompt_tile_of(s):
        return jnp.clip(s - 1, 0, n_p - 1)

    kern = functools.partial(_mixer_kernel, layer=layer, depth=depth, n_prompt_tiles=n_p,
                             tiles_per_seq=tiles_per_seq, seq_len=seq_len, aliased=aliased)
    seq_state_spec = pl.BlockSpec((None, seq_per_step, None, LANES, LANES),
                                  lambda j, s: (layer, prompt_tile_of(s), j, 0, 0))
    in_specs = (
        [pl.BlockSpec((n_meta, D), lambda j, s: (0, 0)),
         pl.BlockSpec((MIX_TILE, D), lambda j, s: (next_tile_of(s), 0)),
         pl.BlockSpec((ROW_TILE, D), lambda j, s: (R // ROW_TILE - 1, 0))]
        + [pl.BlockSpec((None, D, LANES), lambda j, s, g=g: (layer, 0, g * H + weights_head(j, s)))
           for g in range(N_GROUPS)]
        + [pl.BlockSpec((None, LANES, D), lambda j, s: (layer, weights_head(j, s), 0)),
           pl.BlockSpec((None, LANES, D), lambda j, s: (layer, H + weights_head(j, s), 0))]
        + ([pl.BlockSpec(memory_space=pl.ANY)] if aliased else [])
        + [pl.BlockSpec((None, head_params.shape[1], LANES), lambda j, s: (layer, 0, j)),
           seq_state_spec,
           pl.BlockSpec((None, ROW_TILE, LANES), lambda j, s: (layer, 0, j))])
    out_specs = [
        pl.BlockSpec((MIX_TILE, 2 * LANES), lambda j, s: (prompt_tile_of(s), j)),
        pl.BlockSpec((ROW_TILE, 2 * LANES), lambda j, s: (0, j)),
        pl.BlockSpec((n_meta, 2 * LANES), lambda j, s: (0, j)),
        pl.BlockSpec((None, None, LANES, LANES), lambda j, s: (prompt_tile_of(s), j, 0, 0)),
        pl.BlockSpec((None, SUBLANES, LANES), lambda j, s: (prompt_tile_of(s), 0, j)),
        seq_state_spec,
        pl.BlockSpec((ROW_TILE, LANES), lambda j, s: (0, j)),
        pl.BlockSpec((2 * LANES, D), lambda j, s: (j, 0)),
    ]
    out_shape = [
        jax.ShapeDtypeStruct((R - ROW_TILE, 2 * W), _BF16),
        jax.ShapeDtypeStruct((ROW_TILE, 2 * W), _BF16),
        jax.ShapeDtypeStruct((n_meta, 2 * W), _BF16),
        jax.ShapeDtypeStruct((n_p, H, LANES, LANES), _F32),
        jax.ShapeDtypeStruct((n_p, SUBLANES, W), _F32),
        jax.ShapeDtypeStruct(state.shape, _F32),
        jax.ShapeDtypeStruct((ROW_TILE, W), _F32),
        jax.ShapeDtypeStruct((2 * W, D), _BF16),
    ]
    scratch = [
        pltpu.VMEM((D, N_GROUPS * LANES), _BF16),
        pltpu.VMEM((MIX_TILE, N_GROUPS * LANES), _F32),
        pltpu.VMEM((MIX_TILE, N_GROUPS * LANES), _F32),
        pltpu.VMEM((n_meta, N_GROUPS * LANES), _F32),
        pltpu.VMEM((ROW_TILE, N_GROUPS * LANES), _F32),
        pltpu.VMEM((LANES, LANES), _F32),
        pltpu.VMEM((SUBLANES, LANES), _F32),
        pltpu.VMEM((LANES, LANES), _F32),
        pltpu.VMEM((SUBLANES, LANES), _F32),
        pltpu.VMEM((PROMPT_CHUNK, LANES), _F32),
    ] + [pltpu.VMEM((ROW_TILE, LANES), _F32)] * 6 + [
        pltpu.VMEM((ROW_TILE, LANES), _BF16)]
    args = ([hnm, hn, hn] + [w_in] * N_GROUPS + [w_out, w_out] + ([prev_states] if aliased else [])
            + [head_params, state, cpad])
    return pl.pallas_call(
        kern, grid=(H, n_steps), in_specs=in_specs, out_specs=out_specs, out_shape=out_shape,
        scratch_shapes=scratch,
        input_output_aliases={N_GROUPS + 5: 5} if aliased else {},
        compiler_params=pltpu.CompilerParams(dimension_semantics=("arbitrary", "arbitrary"),
                                             vmem_limit_bytes=VMEM_LIMIT),
        name="mixer",
    )(*args)


def _out_kernel(*refs, n_prompt_tiles, first, last):
    refs = list(refs)
    if first:
        hp_ref, hs_ref = refs[:2]
        refs = refs[2:]
    else:
        h_ref = refs.pop(0)
    hm_ref, oy_ref, oys_ref, oym_ref, wo_ref, post_ref = refs[:6]
    refs = refs[6:]
    if last:
        yp_ref, ys_ref, mix_ref = refs
    else:
        pre_ref, hout_ref, hn_ref, hmout_ref, hnm_ref, mix_ref = refs
    i = pl.program_id(0)

    def residual(h, mix):
        return h + mix * _rms_scale(mix) * post_ref[...]

    if not last:
        @pl.when(i == 0)
        def _():
            hm = residual(hm_ref[...], jnp.dot(oym_ref[...], wo_ref[...], preferred_element_type=_F32))
            hmout_ref[...] = hm
            hnm_ref[...] = (hm * _rms_scale(hm) * pre_ref[...]).astype(_BF16)

    mix_ref[...] = jnp.dot(jnp.where(i < n_prompt_tiles, oy_ref[...], oys_ref[...]), wo_ref[...],
                           preferred_element_type=_F32)

    def epilogue(h_src, dst_ref):
        for r0 in range(0, ROW_TILE, OUT_EPILOGUE_ROWS):
            rows = slice(r0, r0 + OUT_EPILOGUE_ROWS)
            h = residual(h_src(rows), mix_ref[rows, :])
            dst_ref[rows, :] = h
            if not last:
                hn_ref[rows, :] = (h * _rms_scale(h) * pre_ref[...]).astype(_BF16)

    if first:
        def h_src(rows):
            return jnp.where(i < n_prompt_tiles, hp_ref[rows, :], hs_ref[rows, :])
    else:
        def h_src(rows):
            return h_ref[rows, :]

    if last:
        @pl.when(i < n_prompt_tiles)
        def _():
            epilogue(h_src, yp_ref)

        @pl.when(i >= n_prompt_tiles)
        def _():
            epilogue(h_src, ys_ref)
    else:
        epilogue(h_src, hout_ref)


def _out(h_args, hm, oy, oys, oym, wo16, post_w, pre_w, *, layer, first, last):
    Rp, W2 = oy.shape
    R = Rp + oys.shape[0]
    D = wo16.shape[1]
    n_meta = oym.shape[0]
    n_p = Rp // ROW_TILE
    kern = functools.partial(_out_kernel, n_prompt_tiles=n_p, first=first, last=last)
    tile_spec = pl.BlockSpec((ROW_TILE, D), lambda i: (i, 0))
    prompt_spec = pl.BlockSpec((ROW_TILE, D), lambda i: (jnp.minimum(i, n_p - 1), 0))
    sample_spec = pl.BlockSpec((ROW_TILE, D), lambda i: (0, 0))
    meta_spec = pl.BlockSpec((n_meta, D), lambda i: (0, 0))
    in_specs = ([prompt_spec, sample_spec] if first else [tile_spec]) + [
        meta_spec,
        pl.BlockSpec((ROW_TILE, W2), lambda i: (jnp.minimum(i, n_p - 1), 0)),
        pl.BlockSpec((ROW_TILE, W2), lambda i: (0, 0)),
        pl.BlockSpec((n_meta, W2), lambda i: (0, 0)),
        pl.BlockSpec((W2, D), lambda i: (0, 0), pipeline_mode=pl.Buffered(1)),
        pl.BlockSpec((None, 1, D), lambda i: (layer, 0, 0)),
    ]
    args = list(h_args) + [hm, oy, oys, oym, wo16, post_w]
    if last:
        out_specs = [prompt_spec, sample_spec]
        out_shape = [jax.ShapeDtypeStruct((n_p * ROW_TILE, D), _F32),
                     jax.ShapeDtypeStruct((ROW_TILE, D), _F32)]
    else:
        in_specs.append(pl.BlockSpec((None, 1, D), lambda i: (layer + 1, 0, 0)))
        args.append(pre_w)
        out_specs = [tile_spec, tile_spec, meta_spec, meta_spec]
        out_shape = [jax.ShapeDtypeStruct((R, D), _F32), jax.ShapeDtypeStruct((R, D), _BF16),
                     jax.ShapeDtypeStruct((n_meta, D), _F32), jax.ShapeDtypeStruct((n_meta, D), _BF16)]
    return pl.pallas_call(
        kern, grid=(n_p + 1,), in_specs=in_specs, out_specs=out_specs, out_shape=out_shape,
        scratch_shapes=[pltpu.VMEM((ROW_TILE, D), _F32)],
        compiler_params=pltpu.CompilerParams(dimension_semantics=("arbitrary",),
                                             vmem_limit_bytes=VMEM_LIMIT),
        name="out_proj",
    )(*args)


def kernel(x_prompt, x_sample, state_hgrn, state_conv, meta_tokens, w_in, conv_w, lb_param,
           hgrn_norm_w, w_out, pre_norm_w, post_norm_w):
    B, T, D = x_prompt.shape
    DB, DT, _ = x_sample.shape
    depth = w_in.shape[0]
    n_meta = meta_tokens.shape[0]
    W = D // 2
    H = W // LANES
    assert W % LANES == 0 and SUBLANES % DT == 0 and n_meta % (2 * SUBLANES) == 0
    assert T % MIX_TILE == 0 and DB * DT == ROW_TILE and n_meta <= PROMPT_CHUNK
    assert w_in.shape == (depth, D, N_GROUPS * W) and state_conv.shape[2] == CONV_W - 1

    pre = pre_norm_w.reshape(depth, 1, D)
    post = post_norm_w.reshape(depth, 1, D)
    n_rows = depth + CONV_W + 1
    head_params = jnp.concatenate(
        [jnp.broadcast_to(lb_param[None], (depth, depth, W)), conv_w,
         jnp.tile(hgrn_norm_w, (1, H))[:, None, :],
         jnp.zeros((depth, -n_rows % SUBLANES, W), _F32)], axis=1)
    xp = x_prompt.reshape(B * T, D)
    xs = x_sample.reshape(DB * DT, D)
    cpad = jnp.concatenate(
        [state_conv, jnp.zeros((depth, DB, DT - (CONV_W - 1), W), _F32)], axis=2
    ).reshape(depth, DB * DT, W)
    tiles_per_seq = T // MIX_TILE
    last_tiles = slice(tiles_per_seq - 1, None, tiles_per_seq)

    hn, hnm = _first_norm(xp, xs, meta_tokens, pre)
    h_args, hm = (xp, xs), meta_tokens
    s_sample = None
    s_list, c_list, u_list = [], [], []
    for layer in range(depth):
        oy, oys, oym, s_p, c_p, s_sample, u_s, wo16 = _mixer(
            hnm, hn, w_in, w_out, head_params, state_hgrn, cpad, s_sample,
            layer=layer, tiles_per_seq=tiles_per_seq, seq_len=DT)
        s_list.append(s_p[last_tiles])
        c_list.append(c_p[last_tiles, SUBLANES - (CONV_W - 1):])
        u_list.append(u_s.reshape(DB, DT, W)[:, DT - (CONV_W - 1):])
        last = layer + 1 == depth
        outs = _out(h_args, hm, oy, oys, oym, wo16, post, pre, layer=layer, first=layer == 0, last=last)
        if not last:
            h_all, hn, hm, hnm = outs
            h_args = (h_all,)
    y_prompt = outs[0].reshape(B, T, D)
    y_sample = outs[1].reshape(DB, DT, D)
    return (y_prompt, y_sample, jnp.stack(s_list), jnp.stack(c_list), s_sample, jnp.stack(u_list))
```
